```python
import math
import jax, jax.numpy as jnp
from jax import lax
import numpy as np

D_MODEL = 1024
BATCH = 8
SEQ = 8192
DEPTH = 1
DEC_BATCH = 2
DEC_SEQ = 8192
PAST_LEN = 128

GRID_W = 64
N_ATTN_HEADS = 8
ATTN_HEAD_DIM = 64
D_ATTN = N_ATTN_HEADS * ATTN_HEAD_DIM
WIN_ROWS_MAX = 8
WIN_COLS = 16
D_SSD = D_MODEL
SSD_HEAD_DIM = 64
N_SSD_HEADS = D_SSD // SSD_HEAD_DIM
SSD_GROUPS = 2
SSD_HEADS_PER_GROUP = N_SSD_HEADS // SSD_GROUPS
SSD_STATE = 128
SSD_CONV_W = 5
SSD_CHUNK = 128
D_XBC = D_SSD + 2 * SSD_GROUPS * SSD_STATE
DT_MIN = 1e-3
DT_MAX = 1e-1
D_MIX = D_ATTN + D_SSD
D_IN = 3 * D_ATTN + D_SSD + D_XBC + 2 * N_SSD_HEADS
D_FF = 2816
FFN_CONV_W = 3
EPS = 1e-6

kernel_name = 'hybrid_natten_ssd_encoder'


def rms_norm(x, g):
    xf = x.astype(jnp.float32)
    y = xf * lax.rsqrt(jnp.mean(xf * xf, axis=-1, keepdims=True) + EPS)
    return (y * g.astype(jnp.float32)).astype(x.dtype)


def depthwise_conv(x, w, b):
    pad = w.shape[0] // 2
    y = lax.conv_general_dilated(x, w[:, None, :].astype(x.dtype), window_strides=(1,),
                                 padding=[(pad, pad)], dimension_numbers=('NWC', 'WIO', 'NWC'),
                                 feature_group_count=x.shape[-1])
    return y + b.astype(x.dtype)


def neighbourhood_attention(q, k, v, rpb):
    b, l, h, dh = q.shape
    rows = l // GRID_W
    kr = min(WIN_ROWS_MAX, rows)
    qg = q.reshape(b, rows, GRID_W, h, dh)
    kg = k.reshape(b, rows, GRID_W, h, dh)
    vg = v.reshape(b, rows, GRID_W, h, dh)
    cols = np.arange(GRID_W)
    col_start = np.clip(cols - WIN_COLS // 2, 0, GRID_W - WIN_COLS)
    col_idx = col_start[:, None] + np.arange(WIN_COLS)[None, :]
    col_off = col_idx - cols[:, None]
    bias_cols = rpb[:, :, col_off + WIN_COLS - 1]
    scale = dh ** -0.5

    def row_block(r):
        rs = jnp.clip(r - kr // 2, 0, rows - kr)
        kb = lax.dynamic_slice_in_dim(kg, rs, kr, axis=1)[:, :, col_idx]
        vb = lax.dynamic_slice_in_dim(vg, rs, kr, axis=1)[:, :, col_idx]
        qr = lax.dynamic_index_in_dim(qg, r, axis=1, keepdims=False)
        row_off = rs + jnp.arange(kr) - r
        bias = jnp.take(bias_cols, row_off + WIN_ROWS_MAX - 1, axis=1)
        bias = bias.transpose(0, 2, 1, 3).astype(jnp.float32)
        s = jnp.einsum('bqhd,brqjhd->bhqrj', qr, kb).astype(jnp.float32) * scale + bias[None]
        p = jax.nn.softmax(s.reshape(b, h, GRID_W, kr * WIN_COLS), axis=-1)
        p = p.reshape(b, h, GRID_W, kr, WIN_COLS).astype(v.dtype)
        return jnp.einsum('bhqrj,brqjhd->bqhd', p, vb)

    out = lax.map(row_block, jnp.arange(rows))
    return out.transpose(1, 0, 2, 3, 4).reshape(b, l, h * dh)


def ssd_scan(x, dt, a, bmat, cmat):
    b, l, g, kh, p = x.shape
    n = bmat.shape[-1]
    q = SSD_CHUNK
    nc = l // q
    xdt = (x * dt[..., None]).reshape(b, nc, q, g, kh, p)
    da = (dt * a).astype(jnp.float32).reshape(b, nc, q, g, kh).transpose(0, 3, 4, 1, 2)
    bc = bmat.reshape(b, nc, q, g, n)
    cc = cmat.reshape(b, nc, q, g, n)
    a_cs = jnp.cumsum(da, axis=-1)
    idx = jnp.arange(q)
    lower = idx[:, None] >= idx[None, :]
    seg = a_cs[..., :, None] - a_cs[..., None, :]
    decay = jnp.exp(jnp.where(lower, seg, -jnp.inf))
    cb = jnp.einsum('bclgn,bcsgn->bgcls', cc, bc)
    y_diag = jnp.einsum('bgcls,bgkcls,bcsgkp->bclgkp', cb, decay, xdt)
    decay_states = jnp.exp(a_cs[..., -1:] - a_cs)
    states = jnp.einsum('bclgn,bgkcl,bclgkp->bcgkpn', bc, decay_states, xdt)
    states = jnp.concatenate([jnp.zeros_like(states[:, :1]), states], axis=1)
    chunk_cs = jnp.cumsum(jnp.pad(a_cs[..., -1], [(0, 0), (0, 0), (0, 0), (1, 0)]), axis=-1)
    zi = jnp.arange(nc + 1)
    seg_c = chunk_cs[..., :, None] - chunk_cs[..., None, :]
    decay_chunk = jnp.exp(jnp.where(zi[:, None] >= zi[None, :], seg_c, -jnp.inf))
    states = jnp.einsum('bgkzc,bcgkpn->bzgkpn', decay_chunk, states)[:, :-1]
    y_off = jnp.einsum('bclgn,bcgkpn,bgkcl->bclgkp', cc, states, jnp.exp(a_cs))
    return (y_diag + y_off).reshape(b, l, g, kh, p)


def ssd_mixer(z, xbc, dt_raw, conv_w, conv_b, dt_bias, a_log, d_skip, norm_g):
    b, l, _ = xbc.shape
    g, kh, p, n = SSD_GROUPS, SSD_HEADS_PER_GROUP, SSD_HEAD_DIM, SSD_STATE
    xbc = jax.nn.silu(depthwise_conv(xbc, conv_w, conv_b))
    xs = xbc[..., :D_SSD].reshape(b, l, g, kh, p)
    bm = xbc[..., D_SSD:D_SSD + g * n].reshape(b, l, g, n)
    cm = xbc[..., D_SSD + g * n:].reshape(b, l, g, n)
    dt = jax.nn.softplus(dt_raw.astype(jnp.float32).reshape(b, l, 2, g, kh)
                         + dt_bias.astype(jnp.float32).reshape(2, g, kh))
    a = -jnp.exp(a_log.astype(jnp.float32)).reshape(2, g, kh)
    flip = lambda t: jnp.flip(t, axis=1)
    y_fwd = ssd_scan(xs, dt[:, :, 0], a[0], bm, cm)
    y_bwd = flip(ssd_scan(flip(xs), flip(dt[:, :, 1]), a[1], flip(bm), flip(cm)))
    y = y_fwd + y_bwd + xs * d_skip.reshape(g, kh)[..., None]
    y = (y.reshape(b, l, D_SSD) * jax.nn.silu(z)).astype(z.dtype)
    y = rms_norm(y.reshape(b, l, g, D_SSD // g), norm_g.reshape(g, D_SSD // g))
    return y.reshape(b, l, D_SSD)


def encoder_layer(x, norm_mix_pre, norm_mix_post, w_in, attn_rpb, attn_out_norm,
                  ssd_conv_w, ssd_conv_b, ssd_dt_bias, ssd_a_log, ssd_d, ssd_norm, w_out,
                  norm_ffn_pre, norm_ffn_post, w_gate, w_up, ffn_conv_w, ffn_conv_b, w_down):
    b, l, _ = x.shape
    h = rms_norm(x, norm_mix_pre)
    proj = h @ w_in
    o = 0
    q = proj[..., o:o + D_ATTN]; o += D_ATTN
    k = proj[..., o:o + D_ATTN]; o += D_ATTN
    v = proj[..., o:o + D_ATTN]; o += D_ATTN
    z = proj[..., o:o + D_SSD]; o += D_SSD
    xbc = proj[..., o:o + D_XBC]; o += D_XBC
    dt_raw = proj[..., o:o + 2 * N_SSD_HEADS]
    hs = (b, l, N_ATTN_HEADS, ATTN_HEAD_DIM)
    attn = neighbourhood_attention(q.reshape(hs), k.reshape(hs), v.reshape(hs), attn_rpb)
    attn = rms_norm(attn, attn_out_norm)
    ssd = ssd_mixer(z, xbc, dt_raw, ssd_conv_w, ssd_conv_b, ssd_dt_bias, ssd_a_log, ssd_d, ssd_norm)
    mix = jnp.concatenate([attn, ssd], axis=-1) @ w_out
    x = x + rms_norm(mix, norm_mix_post)
    h = rms_norm(x, norm_ffn_pre)
    gate = depthwise_conv(h @ w_gate, ffn_conv_w, ffn_conv_b)
    f = (jax.nn.gelu(gate, approximate=True) * (h @ w_up)) @ w_down
    return x + rms_norm(f, norm_ffn_post)


def trunk(x, norm_mix_pre, norm_mix_post, w_in, attn_rpb, attn_out_norm,
          ssd_conv_w, ssd_conv_b, ssd_dt_bias, ssd_a_log, ssd_d, ssd_norm, w_out,
          norm_ffn_pre, norm_ffn_post, w_gate, w_up, ffn_conv_w, ffn_conv_b, w_down):
    for i in range(DEPTH):
        x = encoder_layer(x, norm_mix_pre[i], norm_mix_post[i], w_in[i], attn_rpb[i], attn_out_norm[i],
                          ssd_conv_w[i], ssd_conv_b[i], ssd_dt_bias[i], ssd_a_log[i], ssd_d[i], ssd_norm[i],
                          w_out[i], norm_ffn_pre[i], norm_ffn_post[i], w_gate[i], w_up[i],
                          ffn_conv_w[i], ffn_conv_b[i], w_down[i])
    return x


def setup_inputs(seed: int = 0) -> dict:
    key = jax.random.key(seed)
    ks = jax.random.split(key, 24)
    f32 = jnp.float32

    def nrm(k, shape, scale):
        return jax.random.normal(k, shape, f32) * scale

    def gain(k, shape):
        return 1.0 + 0.05 * jax.random.normal(k, shape, f32)

    dt0 = jnp.exp(jax.random.uniform(ks[12], (DEPTH, 2, N_SSD_HEADS), f32, math.log(DT_MIN), math.log(DT_MAX)))
    return {
        'x_prompt': nrm(ks[0], (BATCH, SEQ, D_MODEL), 1.0),
        'x_sample': nrm(ks[1], (DEC_BATCH, DEC_SEQ, D_MODEL), 1.0),
        'norm_mix_pre': gain(ks[2], (DEPTH, D_MODEL)),
        'norm_mix_post': gain(ks[3], (DEPTH, D_MODEL)),
        'w_in': nrm(ks[4], (DEPTH, D_MODEL, D_IN), D_MODEL ** -0.5),
        'attn_rpb': nrm(ks[5], (DEPTH, N_ATTN_HEADS, 2 * WIN_ROWS_MAX - 1, 2 * WIN_COLS - 1), 0.02),
        'attn_out_norm': gain(ks[6], (DEPTH, D_ATTN)),
        'ssd_conv_w': nrm(ks[7], (DEPTH, SSD_CONV_W, D_XBC), SSD_CONV_W ** -0.5),
        'ssd_conv_b': nrm(ks[8], (DEPTH, D_XBC), 0.02),
        'ssd_dt_bias': dt0 + jnp.log(-jnp.expm1(-dt0)),
        'ssd_a_log': jnp.log(jax.random.uniform(ks[9], (DEPTH, 2, N_SSD_HEADS), f32, 1.0, 16.0)),
        'ssd_d': gain(ks[10], (DEPTH, N_SSD_HEADS)),
        'ssd_norm': gain(ks[11], (DEPTH, D_SSD)),
        'w_out': nrm(ks[13], (DEPTH, D_MIX, D_MODEL), D_MIX ** -0.5),
        'norm_ffn_pre': gain(ks[14], (DEPTH, D_MODEL)),
        'norm_ffn_post': gain(ks[15], (DEPTH, D_MODEL)),
        'w_gate': nrm(ks[16], (DEPTH, D_MODEL, D_FF), D_MODEL ** -0.5),
        'w_up': nrm(ks[17], (DEPTH, D_MODEL, D_FF), D_MODEL ** -0.5),
        'ffn_conv_w': nrm(ks[18], (DEPTH, FFN_CONV_W, D_FF), FFN_CONV_W ** -0.5),
        'ffn_conv_b': nrm(ks[19], (DEPTH, D_FF), 0.02),
        'w_down': nrm(ks[20], (DEPTH, D_FF, D_MODEL), D_FF ** -0.5),
    }


def reference(x_prompt, x_sample, norm_mix_pre, norm_mix_post, w_in, attn_rpb, attn_out_norm,
              ssd_conv_w, ssd_conv_b, ssd_dt_bias, ssd_a_log, ssd_d, ssd_norm, w_out,
              norm_ffn_pre, norm_ffn_post, w_gate, w_up, ffn_conv_w, ffn_conv_b, w_down):
    y_prompt = trunk(x_prompt, norm_mix_pre, norm_mix_post, w_in, attn_rpb, attn_out_norm,
                     ssd_conv_w, ssd_conv_b, ssd_dt_bias, ssd_a_log, ssd_d, ssd_norm, w_out,
                     norm_ffn_pre, norm_ffn_post, w_gate, w_up, ffn_conv_w, ffn_conv_b, w_down)
    y_sample = trunk(x_sample, norm_mix_pre, norm_mix_post, w_in, attn_rpb, attn_out_norm,
                     ssd_conv_w, ssd_conv_b, ssd_dt_bias, ssd_a_log, ssd_d, ssd_norm, w_out,
                     norm_ffn_pre, norm_ffn_post, w_gate, w_up, ffn_conv_w, ffn_conv_b, w_down)
    return (y_prompt, y_sample)
```

```python
import functools

import numpy as np
import jax
import jax.numpy as jnp
from jax import lax
from jax.experimental import pallas as pl
from jax.experimental.pallas import tpu as pltpu

F32 = jnp.float32
BF16 = jnp.bfloat16

D_MODEL = 1024
GRID_W = 64
N_ATTN_HEADS = 8
ATTN_HEAD_DIM = 64
D_ATTN = N_ATTN_HEADS * ATTN_HEAD_DIM
WIN_ROWS = 8
WIN_COLS = 16
D_SSD = 1024
SSD_HEAD_DIM = 64
N_SSD_HEADS = 16
SSD_GROUPS = 2
SSD_STATE = 128
SSD_CONV_W = 5
CHUNK = 128
D_BC = SSD_GROUPS * SSD_STATE
D_XBC = D_SSD + 2 * D_BC
D_FF = 2816
FFN_CONV_W = 3
EPS = 1e-6

LANES = 128
BF16_SUBLANES = 16
MXU_N = 256
VMEM_LIMIT = 56 * 1024 * 1024

TOK_BLOCK = 512
ATT_ROWS = 8
ATT_BLOCK = ATT_ROWS * GRID_W
HEADS_PER_TILE = LANES // ATTN_HEAD_DIM
N_HEAD_TILES = N_ATTN_HEADS // HEADS_PER_TILE
N_BIAS_TYPES = WIN_ROWS
MASK_VALUE = -1e30
HALO = BF16_SUBLANES
DT_REP = LANES // (2 * N_SSD_HEADS)
DIAG_HEADS = 4
FF_COLS = MXU_N


def _params(sem):
    return pltpu.CompilerParams(dimension_semantics=sem, vmem_limit_bytes=VMEM_LIMIT)


def _const_spec(shape):
    nd = len(shape)
    return pl.BlockSpec(shape, lambda *_: (0,) * nd)


def _rms(x, gain):
    return x * lax.rsqrt(jnp.mean(x * x, axis=-1, keepdims=True) + EPS) * gain


def _split3(v):
    hi = v.astype(BF16)
    r1 = v - hi.astype(F32)
    mid = r1.astype(BF16)
    lo = (r1 - mid.astype(F32)).astype(BF16)
    return hi, mid, lo


def _inproj_kernel(x_ref, g_ref, wqkv_ref, wz_ref, wxbc_ref, wdt_ref,
                   qkv_ref, z_ref, xbc_ref, dt_ref):
    h = _rms(x_ref[...], g_ref[...]).astype(BF16)
    qkv_ref[...] = jnp.dot(h, wqkv_ref[...], preferred_element_type=F32).astype(BF16)
    z_ref[...] = jnp.dot(h, wz_ref[...], preferred_element_type=F32).astype(BF16)
    xbc_ref[...] = jnp.dot(h, wxbc_ref[...], preferred_element_type=F32).astype(BF16)
    dt_ref[...] = jnp.dot(h, wdt_ref[...], preferred_element_type=F32)


def _inproj(x2, g, wqkv, wz, wxbc, wdt):
    t = x2.shape[0]
    row = lambda i: (i, 0)
    return pl.pallas_call(
        _inproj_kernel,
        grid=(t // TOK_BLOCK,),
        in_specs=[pl.BlockSpec((TOK_BLOCK, D_MODEL), row), _const_spec(g.shape),
                  _const_spec(wqkv.shape), _const_spec(wz.shape),
                  _const_spec(wxbc.shape), _const_spec(wdt.shape)],
        out_specs=[pl.BlockSpec((TOK_BLOCK, 3 * D_ATTN), row),
                   pl.BlockSpec((TOK_BLOCK, D_SSD), row),
                   pl.BlockSpec((TOK_BLOCK, D_XBC), row),
                   pl.BlockSpec((TOK_BLOCK, LANES), row)],
        out_shape=[jax.ShapeDtypeStruct((t, 3 * D_ATTN), BF16),
                   jax.ShapeDtypeStruct((t, D_SSD), BF16),
                   jax.ShapeDtypeStruct((t, D_XBC), BF16),
                   jax.ShapeDtypeStruct((t, LANES), F32)],
        compiler_params=_params(("parallel",)),
        name="inproj",
    )(x2, g, wqkv, wz, wxbc, wdt)


def _attn_kernel(rows, q_ref, kp_ref, kc_ref, kn_ref, vp_ref, vc_ref, vn_ref,
                 bias_ref, g_ref, o_ref, kwin, vwin):
    j = pl.program_id(1)
    for n, (kr, vr) in enumerate(((kp_ref, vp_ref), (kc_ref, vc_ref), (kn_ref, vn_ref))):
        kwin[n * ATT_BLOCK:(n + 1) * ATT_BLOCK, :] = kr[...]
        vwin[n * ATT_BLOCK:(n + 1) * ATT_BLOCK, :] = vr[...]
    lane = lax.broadcasted_iota(jnp.int32, (GRID_W, LANES), 1)
    first_head = lane < ATTN_HEAD_DIM
    gain = g_ref[...]

    def row_body(i, carry):
        r = j * ATT_ROWS + i
        rs = jnp.clip(r - WIN_ROWS // 2, 0, rows - WIN_ROWS)
        start = pl.multiple_of((rs - (j - 1) * ATT_ROWS) * GRID_W, GRID_W)
        btype = rs - r + (WIN_ROWS - 1)
        qrow = pl.multiple_of(i * GRID_W, GRID_W)
        outs = []
        for hp in range(N_HEAD_TILES):
            cols = slice(hp * LANES, (hp + 1) * LANES)
            q2 = q_ref[pl.ds(qrow, GRID_W), cols]
            zero = jnp.zeros_like(q2)
            qbd = jnp.concatenate([jnp.where(first_head, q2, zero),
                                   jnp.where(first_head, zero, q2)], axis=0)
            kp = kwin[pl.ds(start, WIN_ROWS * GRID_W), cols]
            vp = vwin[pl.ds(start, WIN_ROWS * GRID_W), cols]
            s = lax.dot_general(qbd, kp, (((1,), (1,)), ((), ())),
                                preferred_element_type=F32)
            s = s + bias_ref[btype, hp]
            m = jnp.max(s, axis=-1, keepdims=True)
            p = jnp.exp(s - m)
            denom = jnp.sum(p, axis=-1, keepdims=True)
            o = jnp.dot(p.astype(BF16), vp, preferred_element_type=F32) / denom
            outs.append(jnp.where(first_head, o[:GRID_W], o[GRID_W:]))
        att = jnp.concatenate(outs, axis=1)
        o_ref[pl.ds(qrow, GRID_W), :] = _rms(att, gain).astype(BF16)
        return carry

    lax.fori_loop(0, ATT_ROWS, row_body, 0)


def _attention(qkv, bias_tab, gain, batch, seq):
    t = qkv.shape[0]
    rows = seq // GRID_W
    nblk = seq // ATT_BLOCK

    def kv_spec(shift, col):
        return pl.BlockSpec(
            (ATT_BLOCK, D_ATTN),
            lambda b, j: (b * nblk + jnp.clip(j + shift, 0, nblk - 1), col))

    return pl.pallas_call(
        functools.partial(_attn_kernel, rows),
        grid=(batch, nblk),
        in_specs=[pl.BlockSpec((ATT_BLOCK, D_ATTN), lambda b, j: (b * nblk + j, 0)),
                  kv_spec(-1, 1), kv_spec(0, 1), kv_spec(1, 1),
                  kv_spec(-1, 2), kv_spec(0, 2), kv_spec(1, 2),
                  _const_spec(bias_tab.shape), _const_spec(gain.shape)],
        out_specs=pl.BlockSpec((ATT_BLOCK, D_ATTN), lambda b, j: (b * nblk + j, 0)),
        out_shape=jax.ShapeDtypeStruct((t, D_ATTN), BF16),
        scratch_shapes=[pltpu.VMEM((3 * ATT_BLOCK, D_ATTN), BF16),
                        pltpu.VMEM((3 * ATT_BLOCK, D_ATTN), BF16)],
        compiler_params=_params(("parallel", "arbitrary")),
        name="natten",
    )(qkv, qkv, qkv, qkv, qkv, qkv, qkv, bias_tab, gain)


def _attn_bias_table(rpb):
    o = np.arange(N_BIAS_TYPES)[:, None, None, None]
    kr = np.arange(WIN_ROWS)[None, :, None, None]
    qc = np.arange(GRID_W)[None, None, :, None]
    kc = np.arange(GRID_W)[None, None, None, :]
    cs = np.clip(qc - WIN_COLS // 2, 0, GRID_W - WIN_COLS)
    valid = (kc >= cs) & (kc < cs + WIN_COLS)
    ri = np.broadcast_to(o + kr, (N_BIAS_TYPES, WIN_ROWS, GRID_W, GRID_W))
    ci = np.broadcast_to(np.clip(kc - qc + WIN_COLS - 1, 0, 2 * WIN_COLS - 2), ri.shape)
    vals = rpb.astype(F32)[:, ri, ci]
    vals = jnp.where(jnp.asarray(np.broadcast_to(valid, ri.shape)), vals, MASK_VALUE)
    vals = vals.transpose(1, 0, 3, 2, 4)
    return vals.reshape(N_BIAS_TYPES, N_HEAD_TILES, HEADS_PER_TILE * GRID_W, WIN_ROWS * GRID_W)


def _conv_kernel(nblk, xp_ref, xc_ref, xn_ref, w_ref, b_ref, o_ref, ext):
    i = pl.program_id(0) % nblk
    pad = SSD_CONV_W // 2
    prev = xp_ref[...].astype(F32)
    nxt = xn_ref[...].astype(F32)
    ext[0:HALO, :] = prev * jnp.where(i == 0, 0.0, 1.0)
    ext[HALO:HALO + TOK_BLOCK, :] = xc_ref[...].astype(F32)
    ext[HALO + TOK_BLOCK:, :] = nxt * jnp.where(i == nblk - 1, 0.0, 1.0)
    for c in range(D_XBC // LANES):
        cols = slice(c * LANES, (c + 1) * LANES)
        acc = jnp.zeros((TOK_BLOCK, LANES), F32) + b_ref[:, cols]
        for tap in range(SSD_CONV_W):
            acc = acc + ext[HALO - pad + tap:HALO - pad + tap + TOK_BLOCK, cols] * w_ref[tap:tap + 1, cols]
        o_ref[:, cols] = (acc * (1.0 / (1.0 + jnp.exp(-acc)))).astype(BF16)


def _conv_silu(xbc, w, b, seq):
    t = xbc.shape[0]
    nblk = seq // TOK_BLOCK
    per = TOK_BLOCK // HALO
    last = t // HALO - 1
    return pl.pallas_call(
        functools.partial(_conv_kernel, nblk),
        grid=(t // TOK_BLOCK,),
        in_specs=[pl.BlockSpec((HALO, D_XBC), lambda i: (jnp.maximum(i * per - 1, 0), 0)),
                  pl.BlockSpec((TOK_BLOCK, D_XBC), lambda i: (i, 0)),
                  pl.BlockSpec((HALO, D_XBC), lambda i: (jnp.minimum((i + 1) * per, last), 0)),
                  _const_spec(w.shape), _const_spec(b.shape)],
        out_specs=pl.BlockSpec((TOK_BLOCK, D_XBC), lambda i: (i, 0)),
        out_shape=jax.ShapeDtypeStruct((t, D_XBC), BF16),
        scratch_shapes=[pltpu.VMEM((TOK_BLOCK + 2 * HALO, D_XBC), F32)],
        compiler_params=_params(("parallel",)),
        name="ssd_conv",
    )(xbc, xbc, xbc, w, b)


def _ssd_constants():
    lane = np.arange(LANES)
    m, d, k = lane // 32, (lane // 16) % 2, lane % 16
    tri_l = (np.arange(CHUNK)[:, None] >= np.arange(CHUNK)[None, :]).astype(np.float32)
    tri = np.concatenate([np.tile(tri_l, (1, 3)), np.tile(tri_l.T, (1, 3))], axis=0)
    ecol = np.zeros((LANES, 2 * N_SSD_HEADS * CHUNK), np.float32)
    e64 = np.zeros((LANES, 2 * D_SSD), np.float32)
    for ln in range(LANES):
        if m[ln] < 3:
            c0 = (d[ln] * N_SSD_HEADS + k[ln]) * CHUNK
            ecol[ln, c0:c0 + CHUNK] = 1.0
            c1 = d[ln] * D_SSD + k[ln] * SSD_HEAD_DIM
            e64[ln, c1:c1 + SSD_HEAD_DIM] = 1.0
    return (jnp.asarray(tri, BF16), jnp.asarray(ecol, BF16), jnp.asarray(e64, BF16))


def _pieces_by_replica(v):
    hi, mid, lo = _split3(v)
    rep = lax.broadcasted_iota(jnp.int32, v.shape, 1) >> 5
    zero = jnp.zeros_like(hi)
    return jnp.where(rep == 0, hi, jnp.where(rep == 1, mid, jnp.where(rep == 2, lo, zero)))


def _ssd_dt_prep(dt_ref, dtb_ref, alog_ref, tri_ref):
    x = dt_ref[...] + dtb_ref[...]
    dt = jnp.maximum(x, 0.0) + jnp.log(1.0 + jnp.exp(-jnp.abs(x)))
    da = dt * (-jnp.exp(alog_ref[...]))
    hi, mid, lo = _split3(da)
    cs = jnp.dot(tri_ref[...], jnp.concatenate([hi, mid, lo], axis=0), preferred_element_type=F32)
    bwd_lane = ((lax.broadcasted_iota(jnp.int32, (CHUNK, LANES), 1) >> 4) & 1) == 1
    a = jnp.where(bwd_lane, cs[CHUNK:], cs[:CHUNK])
    return dt, a, bwd_lane


def _chunk_totals(a, bwd_lane):
    return jnp.where(bwd_lane[:1], a[:1], a[CHUNK - 1:])


def _state_step(state_ref, g, b_g, x_g, w_exp_g, decay_row):
    xw = (x_g.astype(F32) * w_exp_g).astype(BF16)
    bt = b_g.astype(F32).T.astype(BF16)
    state_ref[g] = state_ref[g] * decay_row + jnp.dot(bt, xw, preferred_element_type=F32)


def _ssd_bwd_kernel(xbc_ref, dt_ref, dtb_ref, alog_ref, tri_ref, e64_ref, yb_ref, state):
    @pl.when(pl.program_id(1) == 0)
    def _():
        state[...] = jnp.zeros_like(state)

    dt, a, bwd_lane = _ssd_dt_prep(dt_ref, dtb_ref, alog_ref, tri_ref)
    e64 = e64_ref[:, D_SSD:]
    ea_exp = jnp.dot(_pieces_by_replica(jnp.exp(a)), e64, preferred_element_type=F32)
    w = jnp.exp(_chunk_totals(a, bwd_lane) - a) * dt
    w_exp = jnp.dot(_pieces_by_replica(w), e64, preferred_element_type=F32)
    half = D_SSD // SSD_GROUPS
    for g in range(SSD_GROUPS):
        hs = slice(g * half, (g + 1) * half)
        b_g = xbc_ref[:, D_SSD + g * SSD_STATE:D_SSD + (g + 1) * SSD_STATE]
        c_g = xbc_ref[:, D_SSD + D_BC + g * SSD_STATE:D_SSD + D_BC + (g + 1) * SSD_STATE]
        y = jnp.dot(c_g, state[g].astype(BF16), preferred_element_type=F32)
        yb_ref[:, hs] = (y * ea_exp[:, hs]).astype(BF16)
        _state_step(state, g, b_g, xbc_ref[:, hs], w_exp[:, hs], ea_exp[0:1, hs])


def _ssd_fwd_kernel(xbc_ref, dt_ref, z_ref, yb_ref, dtb_ref, alog_ref, dskip_ref, ng_ref,
                    tri_ref, ecol_ref, e64_ref, o_ref, state):
    @pl.when(pl.program_id(1) == 0)
    def _():
        state[...] = jnp.zeros_like(state)

    dt, a, bwd_lane = _ssd_dt_prep(dt_ref, dtb_ref, alog_ref, tri_ref)
    log_dt = jnp.log(dt)
    gq = a - log_dt
    dt_sum = dt + pltpu.roll(dt, 16, axis=1)
    lane = lax.broadcasted_iota(jnp.int32, (CHUNK, LANES), 1)
    rows_t = jnp.where(lane < 32, gq, jnp.log(dt_sum)).T
    acol = jnp.dot(_pieces_by_replica(a), ecol_ref[...], preferred_element_type=F32)
    e64 = e64_ref[:, :D_SSD]
    ea_exp = jnp.dot(_pieces_by_replica(jnp.exp(a)), e64, preferred_element_type=F32)
    w = jnp.exp(_chunk_totals(a, bwd_lane) - gq)
    w_exp = jnp.dot(_pieces_by_replica(w), e64, preferred_element_type=F32)

    li = lax.broadcasted_iota(jnp.int32, (CHUNK, CHUNK), 0)
    si = lax.broadcasted_iota(jnp.int32, (CHUNK, CHUNK), 1)
    below = si < li
    above = si > li
    half = D_SSD // SSD_GROUPS
    heads_per_group = N_SSD_HEADS // SSD_GROUPS
    wide = DIAG_HEADS * SSD_HEAD_DIM
    lane_w = lax.broadcasted_iota(jnp.int32, (CHUNK, wide), 1) >> 6
    bwd0 = N_SSD_HEADS * CHUNK

    y_parts = []
    for g in range(SSD_GROUPS):
        hs = slice(g * half, (g + 1) * half)
        b_g = xbc_ref[:, D_SSD + g * SSD_STATE:D_SSD + (g + 1) * SSD_STATE]
        c_g = xbc_ref[:, D_SSD + D_BC + g * SSD_STATE:D_SSD + D_BC + (g + 1) * SSD_STATE]
        cb = lax.dot_general(c_g, b_g, (((1,), (1,)), ((), ())), preferred_element_type=F32)
        y_off = jnp.dot(c_g, state[g].astype(BF16), preferred_element_type=F32) * ea_exp[:, hs]
        diag = []
        for blk in range(heads_per_group // DIAG_HEADS):
            ms = []
            for hh in range(DIAG_HEADS):
                k = g * heads_per_group + blk * DIAG_HEADS + hh
                arg_f = acol[:, k * CHUNK:(k + 1) * CHUNK] - rows_t[k:k + 1, :]
                arg_b = acol[:, bwd0 + k * CHUNK:bwd0 + (k + 1) * CHUNK] - rows_t[16 + k:17 + k, :]
                arg = jnp.where(below, arg_f, jnp.where(above, arg_b, rows_t[32 + k:33 + k, :]))
                ms.append((jnp.exp(arg) * cb).astype(BF16))
            k0 = g * heads_per_group + blk * DIAG_HEADS
            xq = xbc_ref[:, k0 * SSD_HEAD_DIM:k0 * SSD_HEAD_DIM + wide]
            xbd = jnp.concatenate([jnp.where(lane_w == hh, xq, jnp.zeros_like(xq))
                                   for hh in range(DIAG_HEADS)], axis=0)
            diag.append(jnp.dot(jnp.concatenate(ms, axis=1), xbd, preferred_element_type=F32))
        x_g = xbc_ref[:, hs]
        y = (jnp.concatenate(diag, axis=1) + y_off + yb_ref[:, hs].astype(F32)
             + x_g.astype(F32) * dskip_ref[:, hs])
        zf = z_ref[:, hs].astype(F32)
        y = y * (zf * (1.0 / (1.0 + jnp.exp(-zf))))
        o_ref[:, hs] = _rms(y, ng_ref[:, hs]).astype(BF16)
        _state_step(state, g, b_g, x_g, w_exp[:, hs], ea_exp[CHUNK - 1:CHUNK, hs])


def _ssd(xbcs, dt, z, dt_bias, a_log, dskip, norm_g, consts, batch, seq):
    t = xbcs.shape[0]
    nc = seq // CHUNK
    tri, ecol, e64 = consts
    fwd = lambda b, c: (b * nc + c, 0)
    bwd = lambda b, c: (b * nc + (nc - 1 - c), 0)
    state = pltpu.VMEM((SSD_GROUPS, SSD_STATE, D_SSD // SSD_GROUPS), F32)
    yb = pl.pallas_call(
        _ssd_bwd_kernel,
        grid=(batch, nc),
        in_specs=[pl.BlockSpec((CHUNK, D_XBC), bwd), pl.BlockSpec((CHUNK, LANES), bwd),
                  _const_spec(dt_bias.shape), _const_spec(a_log.shape),
                  _const_spec(tri.shape), _const_spec(e64.shape)],
        out_specs=pl.BlockSpec((CHUNK, D_SSD), bwd),
        out_shape=jax.ShapeDtypeStruct((t, D_SSD), BF16),
        scratch_shapes=[state],
        compiler_params=_params(("parallel", "arbitrary")),
        name="ssd_bwd",
    )(xbcs, dt, dt_bias, a_log, tri, e64)
    return pl.pallas_call(
        _ssd_fwd_kernel,
        grid=(batch, nc),
        in_specs=[pl.BlockSpec((CHUNK, D_XBC), fwd), pl.BlockSpec((CHUNK, LANES), fwd),
                  pl.BlockSpec((CHUNK, D_SSD), fwd), pl.BlockSpec((CHUNK, D_SSD), fwd),
                  _const_spec(dt_bias.shape), _const_spec(a_log.shape),
                  _const_spec(dskip.shape), _const_spec(norm_g.shape),
                  _const_spec(tri.shape), _const_spec(ecol.shape), _const_spec(e64.shape)],
        out_specs=pl.BlockSpec((CHUNK, D_SSD), fwd),
        out_shape=jax.ShapeDtypeStruct((t, D_SSD), BF16),
        scratch_shapes=[state],
        compiler_params=_params(("parallel", "arbitrary")),
        name="ssd_fwd",
    )(xbcs, dt, z, yb, dt_bias, a_log, dskip, norm_g, tri, ecol, e64)


def _outproj_kernel(x_ref, att_ref, ssd_ref, wa_ref, ws_ref, gpost_ref, gpre_ref, x1_ref, h_ref):
    mix = (jnp.dot(att_ref[...], wa_ref[...], preferred_element_type=F32)
           + jnp.dot(ssd_ref[...], ws_ref[...], preferred_element_type=F32))
    x1 = x_ref[...] + _rms(mix, gpost_ref[...])
    x1_ref[...] = x1
    h_ref[...] = _rms(x1, gpre_ref[...]).astype(BF16)


def _outproj(x2, att, ssd, wa, ws, gpost, gpre):
    t = x2.shape[0]
    row = lambda i: (i, 0)
    return pl.pallas_call(
        _outproj_kernel,
        grid=(t // TOK_BLOCK,),
        in_specs=[pl.BlockSpec((TOK_BLOCK, D_MODEL), row), pl.BlockSpec((TOK_BLOCK, D_ATTN), row),
                  pl.BlockSpec((TOK_BLOCK, D_SSD), row), _const_spec(wa.shape), _const_spec(ws.shape),
                  _const_spec(gpost.shape), _const_spec(gpre.shape)],
        out_specs=[pl.BlockSpec((TOK_BLOCK, D_MODEL), row), pl.BlockSpec((TOK_BLOCK, D_MODEL), row)],
        out_shape=[jax.ShapeDtypeStruct((t, D_MODEL), F32), jax.ShapeDtypeStruct((t, D_MODEL), BF16)],
        compiler_params=_params(("parallel",)),
        name="outproj",
    )(x2, att, ssd, wa, ws, gpost, gpre)


def _ffn_kernel(nblk, hp_ref, hc_ref, hn_ref, x1_ref, wg_ref, wu_ref, wd_ref, cw_ref, cb_ref,
                gpost_ref, o_ref, hext, gbuf, act):
    i = pl.program_id(0) % nblk
    hext[0:HALO, :] = hp_ref[...]
    hext[HALO:HALO + TOK_BLOCK, :] = hc_ref[...]
    hext[HALO + TOK_BLOCK:, :] = hn_ref[...]
    keep_prev = jnp.where(i == 0, 0.0, 1.0)
    keep_next = jnp.where(i == nblk - 1, 0.0, 1.0)
    for c in range(D_FF // FF_COLS):
        cols = slice(c * FF_COLS, (c + 1) * FF_COLS)
        gbuf[...] = jnp.dot(hext[...], wg_ref[:, cols], preferred_element_type=F32)
        up = jnp.dot(hc_ref[...], wu_ref[:, cols], preferred_element_type=F32)
        gp = gbuf[HALO - 1:HALO - 1 + TOK_BLOCK, :]
        gc = gbuf[HALO:HALO + TOK_BLOCK, :]
        gn = gbuf[HALO + 1:HALO + 1 + TOK_BLOCK, :]
        row = lax.broadcasted_iota(jnp.int32, (TOK_BLOCK, FF_COLS), 0)
        gp = jnp.where(row == 0, gp * keep_prev, gp)
        gn = jnp.where(row == TOK_BLOCK - 1, gn * keep_next, gn)
        gate = (gp * cw_ref[0:1, cols] + gc * cw_ref[1:2, cols] + gn * cw_ref[2:3, cols]
                + cb_ref[:, cols])
        gelu = 0.5 * gate * (1.0 + jnp.tanh(0.7978845608028654 * (gate + 0.044715 * gate * gate * gate)))
        act[:, cols] = (gelu * up).astype(BF16)
    f = jnp.dot(act[...], wd_ref[...], preferred_element_type=F32)
    o_ref[...] = x1_ref[...] + _rms(f, gpost_ref[...])


def _ffn(h, x1, wg, wu, wd, cw, cb, gpost, seq):
    t = h.shape[0]
    nblk = seq // TOK_BLOCK
    per = TOK_BLOCK // HALO
    last = t // HALO - 1
    row = lambda i: (i, 0)
    return pl.pallas_call(
        functools.partial(_ffn_kernel, nblk),
        grid=(t // TOK_BLOCK,),
        in_specs=[pl.BlockSpec((HALO, D_MODEL), lambda i: (jnp.maximum(i * per - 1, 0), 0)),
                  pl.BlockSpec((TOK_BLOCK, D_MODEL), row),
                  pl.BlockSpec((HALO, D_MODEL), lambda i: (jnp.minimum((i + 1) * per, last), 0)),
                  pl.BlockSpec((TOK_BLOCK, D_MODEL), row),
                  _const_spec(wg.shape), _const_spec(wu.shape), _const_spec(wd.shape),
                  _const_spec(cw.shape), _const_spec(cb.shape), _const_spec(gpost.shape)],
        out_specs=pl.BlockSpec((TOK_BLOCK, D_MODEL), row),
        out_shape=jax.ShapeDtypeStruct((t, D_MODEL), F32),
        scratch_shapes=[pltpu.VMEM((TOK_BLOCK + 2 * HALO, D_MODEL), BF16),
                        pltpu.VMEM((TOK_BLOCK + 2 * HALO, FF_COLS), F32),
                        pltpu.VMEM((TOK_BLOCK, D_FF), BF16)],
        compiler_params=_params(("parallel",)),
        name="ffn",
    )(h, h, h, x1, wg, wu, wd, cw, cb, gpost)


def _dt_tile(v):
    return jnp.tile(v, (1,) * (v.ndim - 1) + (DT_REP,))


def _prepare(norm_mix_pre, norm_mix_post, w_in, attn_rpb, attn_out_norm, ssd_conv_w, ssd_conv_b,
             ssd_dt_bias, ssd_a_log, ssd_d, ssd_norm, w_out, norm_ffn_pre, norm_ffn_post,
             w_gate, w_up, ffn_conv_w, ffn_conv_b, w_down):
    row = lambda v: v.reshape(1, -1).astype(F32)
    o_z = 3 * D_ATTN
    o_x = o_z + D_SSD
    o_dt = o_x + D_XBC
    scale = ATTN_HEAD_DIM ** -0.5
    wqkv = jnp.concatenate([w_in[:, :D_ATTN] * scale, w_in[:, D_ATTN:o_z]], axis=1)
    return dict(
        g_mix_pre=row(norm_mix_pre), g_mix_post=row(norm_mix_post),
        wqkv=wqkv.astype(BF16), wz=w_in[:, o_z:o_x].astype(BF16), wxbc=w_in[:, o_x:o_dt].astype(BF16),
        wdt=_dt_tile(w_in[:, o_dt:]).astype(BF16),
        bias_tab=_attn_bias_table(attn_rpb), g_attn=row(attn_out_norm),
        conv_w=ssd_conv_w.astype(F32), conv_b=row(ssd_conv_b),
        dt_bias=_dt_tile(row(ssd_dt_bias)), a_log=_dt_tile(row(ssd_a_log)),
        dskip=row(jnp.repeat(ssd_d, SSD_HEAD_DIM)), g_ssd=row(ssd_norm),
        wo_a=w_out[:D_ATTN].astype(BF16), wo_s=w_out[D_ATTN:].astype(BF16),
        g_ffn_pre=row(norm_ffn_pre), g_ffn_post=row(norm_ffn_post),
        wg=w_gate.astype(BF16), wu=w_up.astype(BF16), wd=w_down.astype(BF16),
        ffn_cw=ffn_conv_w.astype(F32), ffn_cb=row(ffn_conv_b),
    )


def _layer(x, p, consts):
    batch, seq, _ = x.shape
    x2 = x.reshape(batch * seq, D_MODEL)
    qkv, z, xbc, dt = _inproj(x2, p["g_mix_pre"], p["wqkv"], p["wz"], p["wxbc"], p["wdt"])
    att = _attention(qkv, p["bias_tab"], p["g_attn"], batch, seq)
    xbcs = _conv_silu(xbc, p["conv_w"], p["conv_b"], seq)
    ssd = _ssd(xbcs, dt, z, p["dt_bias"], p["a_log"], p["dskip"], p["g_ssd"], consts, batch, seq)
    x1, h = _outproj(x2, att, ssd, p["wo_a"], p["wo_s"], p["g_mix_post"], p["g_ffn_pre"])
    y = _ffn(h, x1, p["wg"], p["wu"], p["wd"], p["ffn_cw"], p["ffn_cb"], p["g_ffn_post"], seq)
    return y.reshape(batch, seq, D_MODEL)


def kernel(x_prompt, x_sample, norm_mix_pre, norm_mix_post, w_in, attn_rpb, attn_out_norm, ssd_conv_w, ssd_conv_b, ssd_dt_bias, ssd_a_log, ssd_d, ssd_norm, w_out, norm_ffn_pre, norm_ffn_post, w_gate, w_up, ffn_conv_w, ffn_conv_b, w_down):
    assert norm_mix_pre.shape[0] == 1, "single-layer trunk"
    p = _prepare(norm_mix_pre[0], norm_mix_post[0], w_in[0], attn_rpb[0], attn_out_norm[0],
                 ssd_conv_w[0], ssd_conv_b[0], ssd_dt_bias[0], ssd_a_log[0], ssd_d[0], ssd_norm[0],
                 w_out[0], norm_ffn_pre[0], norm_ffn_post[0], w_gate[0], w_up[0],
                 ffn_conv_w[0], ffn_conv_b[0], w_down[0])
    consts = _ssd_constants()
    return (_layer(x_prompt, p, consts), _layer(x_sample, p, consts))
```

```python
import functools

import numpy as np
import jax
import jax.numpy as jnp
from jax import lax
from jax.experimental import pallas as pl
from jax.experimental.pallas import tpu as pltpu

F32 = jnp.float32
BF16 = jnp.bfloat16

D_MODEL = 1024
GRID_W = 64
N_ATTN_HEADS = 8
ATTN_HEAD_DIM = 64
D_ATTN = N_ATTN_HEADS * ATTN_HEAD_DIM
WIN_ROWS = 8
WIN_COLS = 16
D_SSD = 1024
SSD_HEAD_DIM = 64
N_SSD_HEADS = 16
SSD_GROUPS = 2
SSD_STATE = 128
SSD_CONV_W = 5
CHUNK = 128
D_BC = SSD_GROUPS * SSD_STATE
D_XBC = D_SSD + 2 * D_BC
D_FF = 2816
FFN_CONV_W = 3
EPS = 1e-6
LOG2E = 1.4426950408889634

LANES = 128
BF16_SUBLANES = 16
MXU_N = 256
VMEM_LIMIT = 56 * 1024 * 1024

TOK_BLOCK = 512
ATT_ROWS = 8
ATT_BLOCK = ATT_ROWS * GRID_W
HEADS_PER_TILE = LANES // ATTN_HEAD_DIM
N_HEAD_TILES = N_ATTN_HEADS // HEADS_PER_TILE
N_BIAS_TYPES = WIN_ROWS
MASK_VALUE = -1e30
HALO = BF16_SUBLANES
DT_REP = LANES // (2 * N_SSD_HEADS)
DIAG_HEADS = 4
FF_COLS = MXU_N


def _params(sem):
    return pltpu.CompilerParams(dimension_semantics=sem, vmem_limit_bytes=VMEM_LIMIT)


def _const_spec(shape):
    nd = len(shape)
    return pl.BlockSpec(shape, lambda *_: (0,) * nd)


def _rms(x, gain):
    return x * lax.rsqrt(jnp.mean(x * x, axis=-1, keepdims=True) + EPS) * gain


def _split3(v):
    hi = v.astype(BF16)
    r1 = v - hi.astype(F32)
    mid = r1.astype(BF16)
    lo = (r1 - mid.astype(F32)).astype(BF16)
    return hi, mid, lo


def _inproj_kernel(x_ref, g_ref, wqkv_ref, wz_ref, wxbc_ref, wdt_ref,
                   qkv_ref, z_ref, xbc_ref, dt_ref):
    h = _rms(x_ref[...], g_ref[...]).astype(BF16)
    qkv_ref[...] = jnp.dot(h, wqkv_ref[...], preferred_element_type=F32).astype(BF16)
    z_ref[...] = jnp.dot(h, wz_ref[...], preferred_element_type=F32).astype(BF16)
    xbc_ref[...] = jnp.dot(h, wxbc_ref[...], preferred_element_type=F32).astype(BF16)
    dt_ref[...] = jnp.dot(h, wdt_ref[...], preferred_element_type=F32)


def _inproj(x2, g, wqkv, wz, wxbc, wdt):
    t = x2.shape[0]
    row = lambda i: (i, 0)
    return pl.pallas_call(
        _inproj_kernel,
        grid=(t // TOK_BLOCK,),
        in_specs=[pl.BlockSpec((TOK_BLOCK, D_MODEL), row), _const_spec(g.shape),
                  _const_spec(wqkv.shape), _const_spec(wz.shape),
                  _const_spec(wxbc.shape), _const_spec(wdt.shape)],
        out_specs=[pl.BlockSpec((TOK_BLOCK, 3 * D_ATTN), row),
                   pl.BlockSpec((TOK_BLOCK, D_SSD), row),
                   pl.BlockSpec((TOK_BLOCK, D_XBC), row),
                   pl.BlockSpec((TOK_BLOCK, LANES), row)],
        out_shape=[jax.ShapeDtypeStruct((t, 3 * D_ATTN), BF16),
                   jax.ShapeDtypeStruct((t, D_SSD), BF16),
                   jax.ShapeDtypeStruct((t, D_XBC), BF16),
                   jax.ShapeDtypeStruct((t, LANES), F32)],
        compiler_params=_params(("parallel",)),
        name="inproj",
    )(x2, g, wqkv, wz, wxbc, wdt)


def _attn_kernel(rows, q_ref, kp_ref, kc_ref, kn_ref, vp_ref, vc_ref, vn_ref,
                 bias_ref, g_ref, o_ref, kwin, vwin):
    j = pl.program_id(1)
    for n, (kr, vr) in enumerate(((kp_ref, vp_ref), (kc_ref, vc_ref), (kn_ref, vn_ref))):
        kwin[n * ATT_BLOCK:(n + 1) * ATT_BLOCK, :] = kr[...]
        vwin[n * ATT_BLOCK:(n + 1) * ATT_BLOCK, :] = vr[...]
    lane = lax.broadcasted_iota(jnp.int32, (GRID_W, LANES), 1)
    first_head = lane < ATTN_HEAD_DIM
    gain = g_ref[...]
    ones = jnp.ones((WIN_ROWS * GRID_W, LANES), BF16)
    n_keys = WIN_ROWS * GRID_W

    def window(i):
        r = j * ATT_ROWS + i
        rs = jnp.clip(r - WIN_ROWS // 2, 0, rows - WIN_ROWS)
        start = pl.multiple_of((rs - (j - 1) * ATT_ROWS) * GRID_W, GRID_W)
        return start, rs - r + (WIN_ROWS - 1)

    def scores(i, hp, start, btype):
        cols = slice(hp * LANES, (hp + 1) * LANES)
        q2 = q_ref[i * GRID_W:(i + 1) * GRID_W, cols]
        zero = jnp.zeros_like(q2)
        qbd = jnp.concatenate([jnp.where(first_head, q2, zero),
                               jnp.where(first_head, zero, q2)], axis=0)
        kp = kwin[pl.ds(start, n_keys), cols]
        s = lax.dot_general(qbd, kp, (((1,), (1,)), ((), ())), preferred_element_type=F32)
        bias = jnp.concatenate([bias_ref[hp, btype + 2 * t] for t in range(WIN_ROWS // 2)], axis=1)
        return s + bias

    def probs(s):
        return jnp.exp(s - jnp.max(s, axis=-1, keepdims=True)).astype(BF16)

    def weighted(hp, start, p):
        cols = slice(hp * LANES, (hp + 1) * LANES)
        v1 = jnp.concatenate([vwin[pl.ds(start, n_keys), cols], ones], axis=1)
        o = jnp.dot(p, v1, preferred_element_type=F32)
        o = o[:, :LANES] / o[:, LANES:]
        return jnp.where(first_head, o[:GRID_W], o[GRID_W:])

    items = [(i, hp) for i in range(ATT_ROWS) for hp in range(N_HEAD_TILES)]
    wins = [window(i) for i in range(ATT_ROWS)]
    s_val, p_val, outs = {}, {}, {}
    for n in range(len(items) + 2):
        if n < len(items):
            i, hp = items[n]
            s_val[n] = scores(i, hp, *wins[i])
        if 1 <= n <= len(items):
            p_val[n - 1] = probs(s_val.pop(n - 1))
        if n >= 2:
            i, hp = items[n - 2]
            outs[hp] = weighted(hp, wins[i][0], p_val.pop(n - 2))
            if hp == N_HEAD_TILES - 1:
                att = jnp.concatenate([outs[h] for h in range(N_HEAD_TILES)], axis=1)
                o_ref[i * GRID_W:(i + 1) * GRID_W, :] = _rms(att, gain).astype(BF16)


def _attention(qkv, bias_tab, gain, batch, seq):
    t = qkv.shape[0]
    rows = seq // GRID_W
    nblk = seq // ATT_BLOCK

    def kv_spec(shift, col):
        return pl.BlockSpec(
            (ATT_BLOCK, D_ATTN),
            lambda b, j: (b * nblk + jnp.clip(j + shift, 0, nblk - 1), col))

    return pl.pallas_call(
        functools.partial(_attn_kernel, rows),
        grid=(batch, nblk),
        in_specs=[pl.BlockSpec((ATT_BLOCK, D_ATTN), lambda b, j: (b * nblk + j, 0)),
                  kv_spec(-1, 1), kv_spec(0, 1), kv_spec(1, 1),
                  kv_spec(-1, 2), kv_spec(0, 2), kv_spec(1, 2),
                  _const_spec(bias_tab.shape), _const_spec(gain.shape)],
        out_specs=pl.BlockSpec((ATT_BLOCK, D_ATTN), lambda b, j: (b * nblk + j, 0)),
        out_shape=jax.ShapeDtypeStruct((t, D_ATTN), BF16),
        scratch_shapes=[pltpu.VMEM((3 * ATT_BLOCK, D_ATTN), BF16),
                        pltpu.VMEM((3 * ATT_BLOCK, D_ATTN), BF16)],
        compiler_params=_params(("parallel", "arbitrary")),
        name="natten",
    )(qkv, qkv, qkv, qkv, qkv, qkv, qkv, bias_tab, gain)


def _attn_bias_table(rpb):
    pad = GRID_W - WIN_COLS
    padded = jnp.pad(rpb.astype(F32), ((0, 0), (0, 0), (pad, pad)))
    first = WIN_COLS - 1 + pad
    toeplitz = jnp.stack([padded[:, :, first - q:first - q + GRID_W] for q in range(GRID_W)], axis=2)
    qc = np.arange(GRID_W)[:, None]
    kc = np.arange(GRID_W)[None, :]
    cs = np.clip(qc - WIN_COLS // 2, 0, GRID_W - WIN_COLS)
    valid = (kc >= cs) & (kc < cs + WIN_COLS)
    toeplitz = jnp.where(jnp.asarray(valid), toeplitz, MASK_VALUE)
    pair = jnp.concatenate([toeplitz[:, :-1], toeplitz[:, 1:]], axis=-1)
    n_off = pair.shape[1]
    pair = pair.reshape(N_HEAD_TILES, HEADS_PER_TILE, n_off, GRID_W, LANES).transpose(0, 2, 1, 3, 4)
    return pair.reshape(N_HEAD_TILES, n_off, HEADS_PER_TILE * GRID_W, LANES)


CONV_SHIFTS = tuple(s for s in range(-(SSD_CONV_W // 2), SSD_CONV_W // 2 + 1) if s != 0)
CONV_COLS = MXU_N
EXT_ROWS = 2 * CHUNK
SSD_STEP_CHUNKS = 4
SSD_BLOCK = SSD_STEP_CHUNKS * CHUNK


def _ssd_constants():
    lane = np.arange(LANES)
    m, d, k = lane // 32, (lane // 16) % 2, lane % 16
    tri_l = (np.arange(CHUNK)[:, None] >= np.arange(CHUNK)[None, :]).astype(np.float32)
    tri = np.concatenate([np.tile(tri_l, (1, 3)), np.tile(tri_l.T, (1, 3))], axis=0)
    ecol = np.zeros((LANES, 2 * N_SSD_HEADS * CHUNK), np.float32)
    e64 = np.zeros((LANES, 2 * D_SSD), np.float32)
    for ln in range(LANES):
        if m[ln] < 3:
            c0 = (d[ln] * N_SSD_HEADS + k[ln]) * CHUNK
            ecol[ln, c0:c0 + CHUNK] = 1.0
            c1 = d[ln] * D_SSD + k[ln] * SSD_HEAD_DIM
            e64[ln, c1:c1 + SSD_HEAD_DIM] = 1.0
    shift = np.zeros((len(CONV_SHIFTS), CHUNK, EXT_ROWS), np.float32)
    for n, s in enumerate(CONV_SHIFTS):
        shift[n, np.arange(CHUNK), HALO + np.arange(CHUNK) + s] = 1.0
    eye2 = np.concatenate([np.eye(CHUNK), np.eye(CHUNK)], axis=1).astype(np.float32)
    return dict(tri=jnp.asarray(tri, BF16), ecol=jnp.asarray(ecol, BF16), e64=jnp.asarray(e64, BF16),
                shift=jnp.asarray(shift, BF16), eye2=jnp.asarray(eye2, BF16))


def _pieces_by_replica(v):
    hi, mid, lo = _split3(v)
    rep = lax.broadcasted_iota(jnp.int32, v.shape, 1) >> 5
    zero = jnp.zeros_like(hi)
    return jnp.where(rep == 0, hi, jnp.where(rep == 1, mid, jnp.where(rep == 2, lo, zero)))


def _chunk_rows(ci):
    return slice(ci * CHUNK, (ci + 1) * CHUNK)


def _ssd_dt_prep(ci, dt_ref, dtb_ref, alog_ref, tri_ref):
    x = dt_ref[_chunk_rows(ci), :] + dtb_ref[...]
    dt = jnp.maximum(x, 0.0) + jnp.log(1.0 + jnp.exp(-jnp.abs(x)))
    da = dt * (-LOG2E * jnp.exp(alog_ref[...]))
    hi, mid, lo = _split3(da)
    cs = jnp.dot(tri_ref[...], jnp.concatenate([hi, mid, lo], axis=0), preferred_element_type=F32)
    bwd_lane = ((lax.broadcasted_iota(jnp.int32, (CHUNK, LANES), 1) >> 4) & 1) == 1
    a = jnp.where(bwd_lane, cs[CHUNK:], cs[:CHUNK])
    return dt, a, bwd_lane


def _chunk_totals(a, bwd_lane):
    return jnp.where(bwd_lane[:1], a[:1], a[CHUNK - 1:])


def _state_step(state_ref, g, b_g, x_g, w_exp_g, decay_row):
    xw = x_g * w_exp_g.astype(BF16)
    bt = b_g.astype(F32).T.astype(BF16)
    state_ref[g] = state_ref[g] * decay_row + jnp.dot(bt, xw, preferred_element_type=F32)


def _conv_silu_chunk(ci, xc_ref, cw_ref, cbias_ref, shift_ref, ext, xbcs_ref):
    rows = _chunk_rows(ci)
    centre = SSD_CONV_W // 2
    for blk in range(D_XBC // CONV_COLS):
        cols = slice(blk * CONV_COLS, (blk + 1) * CONV_COLS)
        e = ext[ci * CHUNK:ci * CHUNK + EXT_ROWS, cols]
        acc = cbias_ref[:, cols] + xc_ref[rows, cols].astype(F32) * cw_ref[centre:centre + 1, cols]
        for n, s in enumerate(CONV_SHIFTS):
            tap = centre + s
            acc = acc + jnp.dot(shift_ref[n], e, preferred_element_type=F32) * cw_ref[tap:tap + 1, cols]
        xbcs_ref[rows, cols] = (acc * (1.0 / (1.0 + jnp.exp(-acc)))).astype(BF16)


def _ssd_bwd_kernel(nblk, xp_ref, xc_ref, xn_ref, dt_ref, cw_ref, cbias_ref, dtb_ref, alog_ref,
                    tri_ref, e64_ref, shift_ref, xbc_ref, yb_ref, state, ext):
    step = pl.program_id(1)
    blk = nblk - 1 - step

    @pl.when(step == 0)
    def _():
        state[...] = jnp.zeros_like(state)

    keep_prev = jnp.where(blk == 0, 0.0, 1.0)
    keep_next = jnp.where(blk == nblk - 1, 0.0, 1.0)
    ext[0:HALO, :] = (xp_ref[...].astype(F32) * keep_prev).astype(BF16)
    ext[HALO:HALO + SSD_BLOCK, :] = xc_ref[...]
    ext[HALO + SSD_BLOCK:2 * HALO + SSD_BLOCK, :] = (xn_ref[...].astype(F32) * keep_next).astype(BF16)
    ext[2 * HALO + SSD_BLOCK:, :] = jnp.zeros((CHUNK - 2 * HALO, D_XBC), BF16)

    half = D_SSD // SSD_GROUPS

    def prep(ci):
        dt, a, bwd_lane = _ssd_dt_prep(ci, dt_ref, dtb_ref, alog_ref, tri_ref)
        w = jnp.exp2(_chunk_totals(a, bwd_lane) - a) * dt
        return _pieces_by_replica(jnp.exp2(a)), _pieces_by_replica(w)

    order = list(reversed(range(SSD_STEP_CHUNKS)))
    preps = {}
    for ci in order:
        preps[ci] = prep(ci)
        _conv_silu_chunk(ci, xc_ref, cw_ref, cbias_ref, shift_ref, ext, xbc_ref)
    for ci in order:
        rows = _chunk_rows(ci)
        ea_pieces, w_pieces = preps.pop(ci)
        for g in range(SSD_GROUPS):
            hs = slice(g * half, (g + 1) * half)
            bs = slice(D_SSD + g * half, D_SSD + (g + 1) * half)
            b_g = xbc_ref[rows, D_SSD + g * SSD_STATE:D_SSD + (g + 1) * SSD_STATE]
            c_g = xbc_ref[rows, D_SSD + D_BC + g * SSD_STATE:D_SSD + D_BC + (g + 1) * SSD_STATE]
            ea_exp = jnp.dot(ea_pieces, e64_ref[:, bs], preferred_element_type=F32)
            w_exp = jnp.dot(w_pieces, e64_ref[:, bs], preferred_element_type=F32)
            y = jnp.dot(c_g, state[g].astype(BF16), preferred_element_type=F32)
            yb_ref[rows, hs] = (y * ea_exp).astype(BF16)
            _state_step(state, g, b_g, xbc_ref[rows, hs], w_exp, ea_exp[0:1, :])


def _ssd_fwd_kernel(xbc_ref, dt_ref, z_ref, yb_ref, dtb_ref, alog_ref, dskip_ref, ng_ref,
                    tri_ref, ecol_ref, e64_ref, eye2_ref, o_ref, state):
    @pl.when(pl.program_id(1) == 0)
    def _():
        state[...] = jnp.zeros_like(state)

    dskip = dskip_ref[...].astype(BF16)
    lane = lax.broadcasted_iota(jnp.int32, (CHUNK, LANES), 1)
    li = lax.broadcasted_iota(jnp.int32, (CHUNK, CHUNK), 0)
    si = lax.broadcasted_iota(jnp.int32, (CHUNK, CHUNK), 1)
    below = si < li
    above = si > li
    half = D_SSD // SSD_GROUPS
    heads_per_group = N_SSD_HEADS // SSD_GROUPS
    wide = DIAG_HEADS * SSD_HEAD_DIM
    lane_w = lax.broadcasted_iota(jnp.int32, (CHUNK, wide), 1) >> 6
    bwd0 = N_SSD_HEADS * CHUNK

    def prep(ci):
        dt, a, bwd_lane = _ssd_dt_prep(ci, dt_ref, dtb_ref, alog_ref, tri_ref)
        gq = a - LOG2E * jnp.log(dt)
        dt_sum = dt + pltpu.roll(dt, 16, axis=1)
        rows_t = jnp.where(lane < 32, gq, LOG2E * jnp.log(dt_sum)).T
        w = jnp.exp2(_chunk_totals(a, bwd_lane) - gq)
        return rows_t, _pieces_by_replica(a), _pieces_by_replica(jnp.exp2(a)), _pieces_by_replica(w)

    def intra(ci, rows_t, a_pieces):
        rows = _chunk_rows(ci)
        parts = []
        for g in range(SSD_GROUPS):
            b_g = xbc_ref[rows, D_SSD + g * SSD_STATE:D_SSD + (g + 1) * SSD_STATE]
            c_g = xbc_ref[rows, D_SSD + D_BC + g * SSD_STATE:D_SSD + D_BC + (g + 1) * SSD_STATE]
            cb = lax.dot_general(c_g, b_g, (((1,), (1,)), ((), ())), preferred_element_type=F32)
            diag = []
            for blk in range(heads_per_group // DIAG_HEADS):
                ms = []
                k0 = g * heads_per_group + blk * DIAG_HEADS
                for k in range(k0, k0 + DIAG_HEADS, 2):
                    pair = slice(k * CHUNK, (k + 2) * CHUNK)
                    pair_b = slice(bwd0 + k * CHUNK, bwd0 + (k + 2) * CHUNK)
                    acol_f = jnp.dot(a_pieces, ecol_ref[:, pair], preferred_element_type=F32)
                    acol_b = jnp.dot(a_pieces, ecol_ref[:, pair_b], preferred_element_type=F32)
                    for hh in range(2):
                        kk = k + hh
                        cols = slice(hh * CHUNK, (hh + 1) * CHUNK)
                        arg_f = acol_f[:, cols] - rows_t[kk:kk + 1, :]
                        arg_b = acol_b[:, cols] - rows_t[16 + kk:17 + kk, :]
                        arg = jnp.where(below, arg_f, jnp.where(above, arg_b, rows_t[32 + kk:33 + kk, :]))
                        ms.append((jnp.exp2(arg) * cb).astype(BF16))
                qs = slice(k0 * SSD_HEAD_DIM, k0 * SSD_HEAD_DIM + wide)
                xq = xbc_ref[rows, qs]
                lhs = jnp.concatenate(ms + [eye2_ref[...]], axis=1)
                rhs = jnp.concatenate([jnp.where(lane_w == hh, xq, jnp.zeros_like(xq))
                                       for hh in range(DIAG_HEADS)]
                                      + [xq * dskip[:, qs], yb_ref[rows, qs]], axis=0)
                diag.append(jnp.dot(lhs, rhs, preferred_element_type=F32))
            zf = z_ref[rows, g * half:(g + 1) * half].astype(F32)
            parts.append((jnp.concatenate(diag, axis=1), zf * (1.0 / (1.0 + jnp.exp(-zf)))))
        return parts

    def carried(ci, parts, ea_pieces, w_pieces):
        rows = _chunk_rows(ci)
        for g in range(SSD_GROUPS):
            hs = slice(g * half, (g + 1) * half)
            b_g = xbc_ref[rows, D_SSD + g * SSD_STATE:D_SSD + (g + 1) * SSD_STATE]
            c_g = xbc_ref[rows, D_SSD + D_BC + g * SSD_STATE:D_SSD + D_BC + (g + 1) * SSD_STATE]
            y_diag, gate = parts[g]
            ea_exp = jnp.dot(ea_pieces, e64_ref[:, hs], preferred_element_type=F32)
            w_exp = jnp.dot(w_pieces, e64_ref[:, hs], preferred_element_type=F32)
            y_off = jnp.dot(c_g, state[g].astype(BF16), preferred_element_type=F32) * ea_exp
            o_ref[rows, hs] = _rms((y_diag + y_off) * gate, ng_ref[:, hs]).astype(BF16)
            _state_step(state, g, b_g, xbc_ref[rows, hs], w_exp, ea_exp[CHUNK - 1:CHUNK, :])

    preps = {0: prep(0)}
    parts = {0: intra(0, *preps[0][:2])}
    for ci in range(SSD_STEP_CHUNKS):
        if ci + 1 < SSD_STEP_CHUNKS:
            preps[ci + 1] = prep(ci + 1)
            parts[ci + 1] = intra(ci + 1, *preps[ci + 1][:2])
        carried(ci, parts.pop(ci), *preps.pop(ci)[2:])


def _ssd(xbc, dt, z, conv_w, conv_b, dt_bias, a_log, dskip, norm_g, consts, batch, seq):
    t = xbc.shape[0]
    nblk = seq // SSD_BLOCK
    per = SSD_BLOCK // HALO
    last = t // HALO - 1
    tri, ecol, e64, shift, eye2 = (consts[n] for n in ("tri", "ecol", "e64", "shift", "eye2"))
    fwd = lambda b, c: (b * nblk + c, 0)
    bwd = lambda b, c: (b * nblk + (nblk - 1 - c), 0)
    halo_prev = lambda b, c: (jnp.maximum((b * nblk + (nblk - 1 - c)) * per - 1, 0), 0)
    halo_next = lambda b, c: (jnp.minimum((b * nblk + (nblk - c)) * per, last), 0)
    state = pltpu.VMEM((SSD_GROUPS, SSD_STATE, D_SSD // SSD_GROUPS), F32)
    xbcs, yb = pl.pallas_call(
        functools.partial(_ssd_bwd_kernel, nblk),
        grid=(batch, nblk),
        in_specs=[pl.BlockSpec((HALO, D_XBC), halo_prev), pl.BlockSpec((SSD_BLOCK, D_XBC), bwd),
                  pl.BlockSpec((HALO, D_XBC), halo_next), pl.BlockSpec((SSD_BLOCK, LANES), bwd),
                  _const_spec(conv_w.shape), _const_spec(conv_b.shape),
                  _const_spec(dt_bias.shape), _const_spec(a_log.shape),
                  _const_spec(tri.shape), _const_spec(e64.shape), _const_spec(shift.shape)],
        out_specs=[pl.BlockSpec((SSD_BLOCK, D_XBC), bwd), pl.BlockSpec((SSD_BLOCK, D_SSD), bwd)],
        out_shape=[jax.ShapeDtypeStruct((t, D_XBC), BF16), jax.ShapeDtypeStruct((t, D_SSD), BF16)],
        scratch_shapes=[state, pltpu.VMEM((SSD_BLOCK + CHUNK, D_XBC), BF16)],
        compiler_params=_params(("parallel", "arbitrary")),
        name="ssd_bwd",
    )(xbc, xbc, xbc, dt, conv_w, conv_b, dt_bias, a_log, tri, e64, shift)
    return pl.pallas_call(
        _ssd_fwd_kernel,
        grid=(batch, nblk),
        in_specs=[pl.BlockSpec((SSD_BLOCK, D_XBC), fwd), pl.BlockSpec((SSD_BLOCK, LANES), fwd),
                  pl.BlockSpec((SSD_BLOCK, D_SSD), fwd), pl.BlockSpec((SSD_BLOCK, D_SSD), fwd),
                  _const_spec(dt_bias.shape), _const_spec(a_log.shape),
                  _const_spec(dskip.shape), _const_spec(norm_g.shape),
                  _const_spec(tri.shape), _const_spec(ecol.shape), _const_spec(e64.shape),
                  _const_spec(eye2.shape)],
        out_specs=pl.BlockSpec((SSD_BLOCK, D_SSD), fwd),
        out_shape=jax.ShapeDtypeStruct((t, D_SSD), BF16),
        scratch_shapes=[state],
        compiler_params=_params(("parallel", "arbitrary")),
        name="ssd_fwd",
    )(xbcs, dt, z, yb, dt_bias, a_log, dskip, norm_g, tri, ecol, e64, eye2)


def _outproj_kernel(x_ref, att_ref, ssd_ref, wa_ref, ws_ref, gpost_ref, gpre_ref, x1_ref, h_ref):
    mix = (jnp.dot(att_ref[...], wa_ref[...], preferred_element_type=F32)
           + jnp.dot(ssd_ref[...], ws_ref[...], preferred_element_type=F32))
    x1 = x_ref[...] + _rms(mix, gpost_ref[...])
    x1_ref[...] = x1
    h_ref[...] = _rms(x1, gpre_ref[...]).astype(BF16)


def _outproj(x2, att, ssd, wa, ws, gpost, gpre):
    t = x2.shape[0]
    row = lambda i: (i, 0)
    return pl.pallas_call(
        _outproj_kernel,
        grid=(t // TOK_BLOCK,),
        in_specs=[pl.BlockSpec((TOK_BLOCK, D_MODEL), row), pl.BlockSpec((TOK_BLOCK, D_ATTN), row),
                  pl.BlockSpec((TOK_BLOCK, D_SSD), row), _const_spec(wa.shape), _const_spec(ws.shape),
                  _const_spec(gpost.shape), _const_spec(gpre.shape)],
        out_specs=[pl.BlockSpec((TOK_BLOCK, D_MODEL), row), pl.BlockSpec((TOK_BLOCK, D_MODEL), row)],
        out_shape=[jax.ShapeDtypeStruct((t, D_MODEL), F32), jax.ShapeDtypeStruct((t, D_MODEL), BF16)],
        compiler_params=_params(("parallel",)),
        name="outproj",
    )(x2, att, ssd, wa, ws, gpost, gpre)


def _ffn_kernel(nblk, hp_ref, hc_ref, hn_ref, x1_ref, wg_ref, wu_ref, wd_ref, cw_ref, cb_ref,
                gpost_ref, o_ref, hext, gbuf, act):
    i = pl.program_id(0) % nblk
    hext[0:HALO, :] = hp_ref[...]
    hext[HALO:HALO + TOK_BLOCK, :] = hc_ref[...]
    hext[HALO + TOK_BLOCK:, :] = hn_ref[...]
    keep_prev = jnp.where(i == 0, 0.0, 1.0)
    keep_next = jnp.where(i == nblk - 1, 0.0, 1.0)
    for c in range(D_FF // FF_COLS):
        cols = slice(c * FF_COLS, (c + 1) * FF_COLS)
        gbuf[...] = jnp.dot(hext[...], wg_ref[:, cols], preferred_element_type=F32)
        up = jnp.dot(hc_ref[...], wu_ref[:, cols], preferred_element_type=F32)
        gp = gbuf[HALO - 1:HALO - 1 + TOK_BLOCK, :]
        gc = gbuf[HALO:HALO + TOK_BLOCK, :]
        gn = gbuf[HALO + 1:HALO + 1 + TOK_BLOCK, :]
        row = lax.broadcasted_iota(jnp.int32, (TOK_BLOCK, FF_COLS), 0)
        gp = jnp.where(row == 0, gp * keep_prev, gp)
        gn = jnp.where(row == TOK_BLOCK - 1, gn * keep_next, gn)
        gate = (gp * cw_ref[0:1, cols] + gc * cw_ref[1:2, cols] + gn * cw_ref[2:3, cols]
                + cb_ref[:, cols])
        gelu = 0.5 * gate * (1.0 + jnp.tanh(0.7978845608028654 * (gate + 0.044715 * gate * gate * gate)))
        act[:, cols] = (gelu * up).astype(BF16)
    f = jnp.dot(act[...], wd_ref[...], preferred_element_type=F32)
    o_ref[...] = x1_ref[...] + _rms(f, gpost_ref[...])


def _ffn(h, x1, wg, wu, wd, cw, cb, gpost, seq):
    t = h.shape[0]
    nblk = seq // TOK_BLOCK
    per = TOK_BLOCK // HALO
    last = t // HALO - 1
    row = lambda i: (i, 0)
    return pl.pallas_call(
        functools.partial(_ffn_kernel, nblk),
        grid=(t // TOK_BLOCK,),
        in_specs=[pl.BlockSpec((HALO, D_MODEL), lambda i: (jnp.maximum(i * per - 1, 0), 0)),
                  pl.BlockSpec((TOK_BLOCK, D_MODEL), row),
                  pl.BlockSpec((HALO, D_MODEL), lambda i: (jnp.minimum((i + 1) * per, last), 0)),
                  pl.BlockSpec((TOK_BLOCK, D_MODEL), row),
                  _const_spec(wg.shape), _const_spec(wu.shape), _const_spec(wd.shape),
                  _const_spec(cw.shape), _const_spec(cb.shape), _const_spec(gpost.shape)],
        out_specs=pl.BlockSpec((TOK_BLOCK, D_MODEL), row),
        out_shape=jax.ShapeDtypeStruct((t, D_MODEL), F32),
        scratch_shapes=[pltpu.VMEM((TOK_BLOCK + 2 * HALO, D_MODEL), BF16),
                        pltpu.VMEM((TOK_BLOCK + 2 * HALO, FF_COLS), F32),
                        pltpu.VMEM((TOK_BLOCK, D_FF), BF16)],
        compiler_params=_params(("parallel",)),
        name="ffn",
    )(h, h, h, x1, wg, wu, wd, cw, cb, gpost)


def _dt_tile(v):
    return jnp.tile(v, (1,) * (v.ndim - 1) + (DT_REP,))


def _prepare(norm_mix_pre, norm_mix_post, w_in, attn_rpb, attn_out_norm, ssd_conv_w, ssd_conv_b,
             ssd_dt_bias, ssd_a_log, ssd_d, ssd_norm, w_out, norm_ffn_pre, norm_ffn_post,
             w_gate, w_up, ffn_conv_w, ffn_conv_b, w_down):
    row = lambda v: v.reshape(1, -1).astype(F32)
    o_z = 3 * D_ATTN
    o_x = o_z + D_SSD
    o_dt = o_x + D_XBC
    scale = ATTN_HEAD_DIM ** -0.5
    wqkv = jnp.concatenate([w_in[:, :D_ATTN] * scale, w_in[:, D_ATTN:o_z]], axis=1)
    return dict(
        g_mix_pre=row(norm_mix_pre), g_mix_post=row(norm_mix_post),
        wqkv=wqkv.astype(BF16), wz=w_in[:, o_z:o_x].astype(BF16), wxbc=w_in[:, o_x:o_dt].astype(BF16),
        wdt=_dt_tile(w_in[:, o_dt:]).astype(BF16),
        bias_tab=_attn_bias_table(attn_rpb), g_attn=row(attn_out_norm),
        conv_w=ssd_conv_w.astype(F32), conv_b=row(ssd_conv_b),
        dt_bias=_dt_tile(row(ssd_dt_bias)), a_log=_dt_tile(row(ssd_a_log)),
        dskip=row(jnp.repeat(ssd_d, SSD_HEAD_DIM)), g_ssd=row(ssd_norm),
        wo_a=w_out[:D_ATTN].astype(BF16), wo_s=w_out[D_ATTN:].astype(BF16),
        g_ffn_pre=row(norm_ffn_pre), g_ffn_post=row(norm_ffn_post),
        wg=w_gate.astype(BF16), wu=w_up.astype(BF16), wd=w_down.astype(BF16),
        ffn_cw=ffn_conv_w.astype(F32), ffn_cb=row(ffn_conv_b),
    )


def _layer(x, p, consts):
    batch, seq, _ = x.shape
    x2 = x.reshape(batch * seq, D_MODEL)
    qkv, z, xbc, dt = _inproj(x2, p["g_mix_pre"], p["wqkv"], p["wz"], p["wxbc"], p["wdt"])
    att = _attention(qkv, p["bias_tab"], p["g_attn"], batch, seq)
    ssd = _ssd(xbc, dt, z, p["conv_w"], p["conv_b"], p["dt_bias"], p["a_log"], p["dskip"], p["g_ssd"],
               consts, batch, seq)
    x1, h = _outproj(x2, att, ssd, p["wo_a"], p["wo_s"], p["g_mix_post"], p["g_ffn_pre"])
    y = _ffn(h, x1, p["wg"], p["wu"], p["wd"], p["ffn_cw"], p["ffn_cb"], p["g_ffn_post"], seq)
    return y.reshape(batch, seq, D_MODEL)


def kernel(x_prompt, x_sample, norm_mix_pre, norm_mix_post, w_in, attn_rpb, attn_out_norm, ssd_conv_w, ssd_conv_b, ssd_dt_bias, ssd_a_log, ssd_d, ssd_norm, w_out, norm_ffn_pre, norm_ffn_post, w_gate, w_up, ffn_conv_w, ffn_conv_b, w_down):
    assert norm_mix_pre.shape[0] == 1, "single-layer trunk"
    p = _prepare(norm_mix_pre[0], norm_mix_post[0], w_in[0], attn_rpb[0], attn_out_norm[0],
                 ssd_conv_w[0], ssd_conv_b[0], ssd_dt_bias[0], ssd_a_log[0], ssd_d[0], ssd_norm[0],
                 w_out[0], norm_ffn_pre[0], norm_ffn_post[0], w_gate[0], w_up[0],
                 ffn_conv_w[0], ffn_conv_b[0], w_down[0])
    consts = _ssd_constants()
    return (_layer(x_prompt, p, consts), _layer(x_sample, p, consts))
```

```python
import functools

import numpy as np
import jax
import jax.numpy as jnp
from jax import lax
from jax.experimental import pallas as pl
from jax.experimental.pallas import tpu as pltpu

F32 = jnp.float32
BF16 = jnp.bfloat16

D_MODEL = 1024
GRID_W = 64
N_ATTN_HEADS = 8
ATTN_HEAD_DIM = 64
D_ATTN = N_ATTN_HEADS * ATTN_HEAD_DIM
WIN_ROWS = 8
WIN_COLS = 16
D_SSD = 1024
SSD_HEAD_DIM = 64
N_SSD_HEADS = 16
SSD_GROUPS = 2
SSD_STATE = 128
SSD_CONV_W = 5
CHUNK = 128
D_BC = SSD_GROUPS * SSD_STATE
D_XBC = D_SSD + 2 * D_BC
D_FF = 2816
FFN_CONV_W = 3
EPS = 1e-6
LOG2E = 1.4426950408889634

LANES = 128
BF16_SUBLANES = 16
MXU_N = 256
VMEM_LIMIT = 56 * 1024 * 1024

TOK_BLOCK = 512
ATT_ROWS = 8
ATT_BLOCK = ATT_ROWS * GRID_W
HEADS_PER_TILE = LANES // ATTN_HEAD_DIM
N_HEAD_TILES = N_ATTN_HEADS // HEADS_PER_TILE
N_BIAS_TYPES = WIN_ROWS
MASK_VALUE = -1e30
HALO = BF16_SUBLANES
DT_REP = LANES // (2 * N_SSD_HEADS)
DIAG_HEADS = 4
FF_COLS = MXU_N


def _params(sem):
    return pltpu.CompilerParams(dimension_semantics=sem, vmem_limit_bytes=VMEM_LIMIT)


def _const_spec(shape):
    nd = len(shape)
    return pl.BlockSpec(shape, lambda *_: (0,) * nd)


def _rms(x, gain):
    return x * lax.rsqrt(jnp.mean(x * x, axis=-1, keepdims=True) + EPS) * gain


def _split3(v):
    hi = v.astype(BF16)
    r1 = v - hi.astype(F32)
    mid = r1.astype(BF16)
    lo = (r1 - mid.astype(F32)).astype(BF16)
    return hi, mid, lo


def _inproj_kernel(x_ref, g_ref, wqkv_ref, wz_ref, wxbc_ref, wdt_ref,
                   qkv_ref, z_ref, xbc_ref, dt_ref):
    h = _rms(x_ref[...], g_ref[...]).astype(BF16)
    qkv_ref[...] = jnp.dot(h, wqkv_ref[...], preferred_element_type=F32).astype(BF16)
    z_ref[...] = jnp.dot(h, wz_ref[...], preferred_element_type=F32).astype(BF16)
    xbc_ref[...] = jnp.dot(h, wxbc_ref[...], preferred_element_type=F32).astype(BF16)
    dt_ref[...] = jnp.dot(h, wdt_ref[...], preferred_element_type=F32)


def _inproj(x2, g, wqkv, wz, wxbc, wdt):
    t = x2.shape[0]
    row = lambda i: (i, 0)
    return pl.pallas_call(
        _inproj_kernel,
        grid=(t // TOK_BLOCK,),
        in_specs=[pl.BlockSpec((TOK_BLOCK, D_MODEL), row), _const_spec(g.shape),
                  _const_spec(wqkv.shape), _const_spec(wz.shape),
                  _const_spec(wxbc.shape), _const_spec(wdt.shape)],
        out_specs=[pl.BlockSpec((TOK_BLOCK, 3 * D_ATTN), row),
                   pl.BlockSpec((TOK_BLOCK, D_SSD), row),
                   pl.BlockSpec((TOK_BLOCK, D_XBC), row),
                   pl.BlockSpec((TOK_BLOCK, LANES), row)],
        out_shape=[jax.ShapeDtypeStruct((t, 3 * D_ATTN), BF16),
                   jax.ShapeDtypeStruct((t, D_SSD), BF16),
                   jax.ShapeDtypeStruct((t, D_XBC), BF16),
                   jax.ShapeDtypeStruct((t, LANES), F32)],
        compiler_params=_params(("parallel",)),
        name="inproj",
    )(x2, g, wqkv, wz, wxbc, wdt)


def _attn_kernel(rows, q_ref, kp_ref, kc_ref, kn_ref, vp_ref, vc_ref, vn_ref,
                 bias_ref, g_ref, o_ref, kwin, vwin):
    j = pl.program_id(1)
    for n, (kr, vr) in enumerate(((kp_ref, vp_ref), (kc_ref, vc_ref), (kn_ref, vn_ref))):
        kwin[n * ATT_BLOCK:(n + 1) * ATT_BLOCK, :] = kr[...]
        vwin[n * ATT_BLOCK:(n + 1) * ATT_BLOCK, :] = vr[...]
    lane = lax.broadcasted_iota(jnp.int32, (GRID_W, LANES), 1)
    first_head = lane < ATTN_HEAD_DIM
    gain = g_ref[...]
    ones = jnp.ones((WIN_ROWS * GRID_W, LANES), BF16)
    n_keys = WIN_ROWS * GRID_W

    def window(i):
        r = j * ATT_ROWS + i
        rs = jnp.clip(r - WIN_ROWS // 2, 0, rows - WIN_ROWS)
        start = pl.multiple_of((rs - (j - 1) * ATT_ROWS) * GRID_W, GRID_W)
        return start, rs - r + (WIN_ROWS - 1)

    def scores(i, hp, start, btype):
        cols = slice(hp * LANES, (hp + 1) * LANES)
        q2 = q_ref[i * GRID_W:(i + 1) * GRID_W, cols]
        zero = jnp.zeros_like(q2)
        qbd = jnp.concatenate([jnp.where(first_head, q2, zero),
                               jnp.where(first_head, zero, q2)], axis=0)
        kp = kwin[pl.ds(start, n_keys), cols]
        s = lax.dot_general(qbd, kp, (((1,), (1,)), ((), ())), preferred_element_type=F32)
        bias = jnp.concatenate([bias_ref[hp, btype + 2 * t] for t in range(WIN_ROWS // 2)], axis=1)
        return s + bias

    def probs(s):
        return jnp.exp(s - jnp.max(s, axis=-1, keepdims=True)).astype(BF16)

    def weighted(hp, start, p):
        cols = slice(hp * LANES, (hp + 1) * LANES)
        v1 = jnp.concatenate([vwin[pl.ds(start, n_keys), cols], ones], axis=1)
        o = jnp.dot(p, v1, preferred_element_type=F32)
        o = o[:, :LANES] / o[:, LANES:]
        return jnp.where(first_head, o[:GRID_W], o[GRID_W:])

    items = [(i, hp) for i in range(ATT_ROWS) for hp in range(N_HEAD_TILES)]
    wins = [window(i) for i in range(ATT_ROWS)]
    s_val, p_val, outs = {}, {}, {}
    for n in range(len(items) + 2):
        if n < len(items):
            i, hp = items[n]
            s_val[n] = scores(i, hp, *wins[i])
        if 1 <= n <= len(items):
            p_val[n - 1] = probs(s_val.pop(n - 1))
        if n >= 2:
            i, hp = items[n - 2]
            outs[hp] = weighted(hp, wins[i][0], p_val.pop(n - 2))
            if hp == N_HEAD_TILES - 1:
                att = jnp.concatenate([outs[h] for h in range(N_HEAD_TILES)], axis=1)
                o_ref[i * GRID_W:(i + 1) * GRID_W, :] = _rms(att, gain).astype(BF16)


def _attention(qkv, bias_tab, gain, batch, seq):
    t = qkv.shape[0]
    rows = seq // GRID_W
    nblk = seq // ATT_BLOCK

    def kv_spec(shift, col):
        return pl.BlockSpec(
            (ATT_BLOCK, D_ATTN),
            lambda b, j: (b * nblk + jnp.clip(j + shift, 0, nblk - 1), col))

    return pl.pallas_call(
        functools.partial(_attn_kernel, rows),
        grid=(batch, nblk),
        in_specs=[pl.BlockSpec((ATT_BLOCK, D_ATTN), lambda b, j: (b * nblk + j, 0)),
                  kv_spec(-1, 1), kv_spec(0, 1), kv_spec(1, 1),
                  kv_spec(-1, 2), kv_spec(0, 2), kv_spec(1, 2),
                  _const_spec(bias_tab.shape), _const_spec(gain.shape)],
        out_specs=pl.BlockSpec((ATT_BLOCK, D_ATTN), lambda b, j: (b * nblk + j, 0)),
        out_shape=jax.ShapeDtypeStruct((t, D_ATTN), BF16),
        scratch_shapes=[pltpu.VMEM((3 * ATT_BLOCK, D_ATTN), BF16),
                        pltpu.VMEM((3 * ATT_BLOCK, D_ATTN), BF16)],
        compiler_params=_params(("parallel", "arbitrary")),
        name="natten",
    )(qkv, qkv, qkv, qkv, qkv, qkv, qkv, bias_tab, gain)


def _attn_bias_table(rpb):
    pad = GRID_W - WIN_COLS
    padded = jnp.pad(rpb.astype(F32), ((0, 0), (0, 0), (pad, pad)))
    first = WIN_COLS - 1 + pad
    toeplitz = jnp.stack([padded[:, :, first - q:first - q + GRID_W] for q in range(GRID_W)], axis=2)
    qc = np.arange(GRID_W)[:, None]
    kc = np.arange(GRID_W)[None, :]
    cs = np.clip(qc - WIN_COLS // 2, 0, GRID_W - WIN_COLS)
    valid = (kc >= cs) & (kc < cs + WIN_COLS)
    toeplitz = jnp.where(jnp.asarray(valid), toeplitz, MASK_VALUE)
    pair = jnp.concatenate([toeplitz[:, :-1], toeplitz[:, 1:]], axis=-1)
    n_off = pair.shape[1]
    pair = pair.reshape(N_HEAD_TILES, HEADS_PER_TILE, n_off, GRID_W, LANES).transpose(0, 2, 1, 3, 4)
    return pair.reshape(N_HEAD_TILES, n_off, HEADS_PER_TILE * GRID_W, LANES)


CONV_SHIFTS = tuple(s for s in range(-(SSD_CONV_W // 2), SSD_CONV_W // 2 + 1) if s != 0)
CONV_COLS = MXU_N
EXT_ROWS = 2 * CHUNK
SSD_STEP_CHUNKS = 8
SSD_BLOCK = SSD_STEP_CHUNKS * CHUNK
CONV_LOOKAHEAD = 4
FWD_LOOKAHEAD = 4


def _ssd_constants():
    lane = np.arange(LANES)
    m, d, k = lane // 32, (lane // 16) % 2, lane % 16
    tri_l = (np.arange(CHUNK)[:, None] >= np.arange(CHUNK)[None, :]).astype(np.float32)
    tri = np.concatenate([np.tile(tri_l, (1, 3)), np.tile(tri_l.T, (1, 3))], axis=0)
    ecol = np.zeros((LANES, 2 * N_SSD_HEADS * CHUNK), np.float32)
    e64 = np.zeros((LANES, 2 * D_SSD), np.float32)
    for ln in range(LANES):
        if m[ln] < 3:
            c0 = (d[ln] * N_SSD_HEADS + k[ln]) * CHUNK
            ecol[ln, c0:c0 + CHUNK] = 1.0
            c1 = d[ln] * D_SSD + k[ln] * SSD_HEAD_DIM
            e64[ln, c1:c1 + SSD_HEAD_DIM] = 1.0
    shift = np.zeros((len(CONV_SHIFTS), CHUNK, EXT_ROWS), np.float32)
    for n, s in enumerate(CONV_SHIFTS):
        shift[n, np.arange(CHUNK), HALO + np.arange(CHUNK) + s] = 1.0
    eye2 = np.concatenate([np.eye(CHUNK), np.eye(CHUNK)], axis=1).astype(np.float32)
    headmask = (np.arange(DIAG_HEADS)[:, None] == np.arange(DIAG_HEADS * SSD_HEAD_DIM)[None, :] // SSD_HEAD_DIM)
    return dict(tri=jnp.asarray(tri, BF16), ecol=jnp.asarray(ecol, BF16), e64=jnp.asarray(e64, BF16),
                shift=jnp.asarray(shift, BF16), eye2=jnp.asarray(eye2, BF16),
                headmask=jnp.asarray(headmask.astype(np.float32), BF16))


def _pieces_by_replica(v):
    hi, mid, lo = _split3(v)
    rep = lax.broadcasted_iota(jnp.int32, v.shape, 1) >> 5
    zero = jnp.zeros_like(hi)
    return jnp.where(rep == 0, hi, jnp.where(rep == 1, mid, jnp.where(rep == 2, lo, zero)))


def _chunk_rows(ci):
    return slice(ci * CHUNK, (ci + 1) * CHUNK)


def _ssd_dt_prep(ci, dt_ref, dtb_ref, alog_ref, tri_ref):
    x = dt_ref[_chunk_rows(ci), :] + dtb_ref[...]
    dt = jnp.maximum(x, 0.0) + jnp.log(1.0 + jnp.exp(-jnp.abs(x)))
    da = dt * (-LOG2E * jnp.exp(alog_ref[...]))
    hi, mid, lo = _split3(da)
    cs = jnp.dot(tri_ref[...], jnp.concatenate([hi, mid, lo], axis=0), preferred_element_type=F32)
    bwd_lane = ((lax.broadcasted_iota(jnp.int32, (CHUNK, LANES), 1) >> 4) & 1) == 1
    a = jnp.where(bwd_lane, cs[CHUNK:], cs[:CHUNK])
    return dt, a, bwd_lane


def _chunk_totals(a, bwd_lane):
    return jnp.where(bwd_lane[:1], a[:1], a[CHUNK - 1:])


def _state_step(state_ref, g, b_g, x_g, w_exp_g, decay_row):
    xw = x_g * w_exp_g.astype(BF16)
    bt = b_g.astype(F32).T.astype(BF16)
    state_ref[g] = state_ref[g] * decay_row + jnp.dot(bt, xw, preferred_element_type=F32)


def _conv_shifted(ci, blk, shift_ref, ext):
    cols = slice(blk * CONV_COLS, (blk + 1) * CONV_COLS)
    e = ext[ci * CHUNK:ci * CHUNK + EXT_ROWS, cols]
    return [jnp.dot(shift_ref[n], e, preferred_element_type=F32) for n in range(len(CONV_SHIFTS))]


def _conv_silu_block(ci, blk, shifted, xc_ref, cw_ref, cbias_ref, xbcs_ref):
    rows = _chunk_rows(ci)
    cols = slice(blk * CONV_COLS, (blk + 1) * CONV_COLS)
    centre = SSD_CONV_W // 2
    acc = cbias_ref[:, cols] + xc_ref[rows, cols].astype(F32) * cw_ref[centre:centre + 1, cols]
    for s, x_s in zip(CONV_SHIFTS, shifted):
        acc = acc + x_s * cw_ref[centre + s:centre + s + 1, cols]
    xbcs_ref[rows, cols] = (acc * (1.0 / (1.0 + jnp.exp(-acc)))).astype(BF16)


def _ssd_bwd_kernel(nblk, xp_ref, xc_ref, xn_ref, dt_ref, cw_ref, cbias_ref, dtb_ref, alog_ref,
                    tri_ref, e64_ref, shift_ref, xbc_ref, yb_ref, state, ext):
    step = pl.program_id(1)
    blk = nblk - 1 - step

    @pl.when(step == 0)
    def _():
        state[...] = jnp.zeros_like(state)

    keep_prev = jnp.where(blk == 0, 0.0, 1.0)
    keep_next = jnp.where(blk == nblk - 1, 0.0, 1.0)
    ext[0:HALO, :] = (xp_ref[...].astype(F32) * keep_prev).astype(BF16)
    ext[HALO:HALO + SSD_BLOCK, :] = xc_ref[...]
    ext[HALO + SSD_BLOCK:2 * HALO + SSD_BLOCK, :] = (xn_ref[...].astype(F32) * keep_next).astype(BF16)
    ext[2 * HALO + SSD_BLOCK:, :] = jnp.zeros((CHUNK - 2 * HALO, D_XBC), BF16)

    half = D_SSD // SSD_GROUPS

    def prep(ci):
        dt, a, bwd_lane = _ssd_dt_prep(ci, dt_ref, dtb_ref, alog_ref, tri_ref)
        w = jnp.exp2(_chunk_totals(a, bwd_lane) - a) * dt
        return _pieces_by_replica(jnp.exp2(a)), _pieces_by_replica(w)

    def state_part(ci, ea_pieces, w_pieces):
        rows = _chunk_rows(ci)
        for g in range(SSD_GROUPS):
            hs = slice(g * half, (g + 1) * half)
            bs = slice(D_SSD + g * half, D_SSD + (g + 1) * half)
            b_g = xbc_ref[rows, D_SSD + g * SSD_STATE:D_SSD + (g + 1) * SSD_STATE]
            c_g = xbc_ref[rows, D_SSD + D_BC + g * SSD_STATE:D_SSD + D_BC + (g + 1) * SSD_STATE]
            ea_exp = jnp.dot(ea_pieces, e64_ref[:, bs], preferred_element_type=F32)
            w_exp = jnp.dot(w_pieces, e64_ref[:, bs], preferred_element_type=F32)
            y = jnp.dot(c_g, state[g].astype(BF16), preferred_element_type=F32)
            yb_ref[rows, hs] = (y * ea_exp).astype(BF16)
            _state_step(state, g, b_g, xbc_ref[rows, hs], w_exp, ea_exp[0:1, :])

    order = list(reversed(range(SSD_STEP_CHUNKS)))
    preps = {ci: prep(ci) for ci in order}
    n_col = D_XBC // CONV_COLS
    items = [(ci, cblk) for ci in order for cblk in range(n_col)]
    shifted = {}
    for n in range(len(items) + CONV_LOOKAHEAD):
        if n < len(items):
            shifted[n] = _conv_shifted(*items[n], shift_ref, ext)
        m = n - CONV_LOOKAHEAD
        if m >= 0:
            ci, cblk = items[m]
            _conv_silu_block(ci, cblk, shifted.pop(m), xc_ref, cw_ref, cbias_ref, xbc_ref)
            if cblk == n_col - 1:
                state_part(ci, *preps.pop(ci))


def _ssd_fwd_kernel(xbc_ref, dt_ref, z_ref, yb_ref, dtb_ref, alog_ref, dskip_ref, ng_ref,
                    tri_ref, ecol_ref, e64_ref, eye2_ref, headmask_ref, o_ref, state, pieces):
    @pl.when(pl.program_id(1) == 0)
    def _():
        state[...] = jnp.zeros_like(state)

    dskip = dskip_ref[...].astype(BF16)
    lane = lax.broadcasted_iota(jnp.int32, (CHUNK, LANES), 1)
    li = lax.broadcasted_iota(jnp.int32, (CHUNK, CHUNK), 0)
    si = lax.broadcasted_iota(jnp.int32, (CHUNK, CHUNK), 1)
    below = si < li
    above = si > li
    half = D_SSD // SSD_GROUPS
    heads_per_group = N_SSD_HEADS // SSD_GROUPS
    wide = DIAG_HEADS * SSD_HEAD_DIM
    bwd0 = N_SSD_HEADS * CHUNK

    def prep(ci):
        dt, a, bwd_lane = _ssd_dt_prep(ci, dt_ref, dtb_ref, alog_ref, tri_ref)
        gq = a - LOG2E * jnp.log(dt)
        dt_sum = dt + pltpu.roll(dt, 16, axis=1)
        rows_t = jnp.where(lane < 32, gq, LOG2E * jnp.log(dt_sum)).T
        w = jnp.exp2(_chunk_totals(a, bwd_lane) - gq)
        rows = _chunk_rows(ci)
        cbs = []
        for g in range(SSD_GROUPS):
            b_g = xbc_ref[rows, D_SSD + g * SSD_STATE:D_SSD + (g + 1) * SSD_STATE]
            c_g = xbc_ref[rows, D_SSD + D_BC + g * SSD_STATE:D_SSD + D_BC + (g + 1) * SSD_STATE]
            cbs.append(lax.dot_general(c_g, b_g, (((1,), (1,)), ((), ())), preferred_element_type=F32))
        pieces[ci, 0] = _pieces_by_replica(a)
        pieces[ci, 1] = _pieces_by_replica(jnp.exp2(a))
        pieces[ci, 2] = _pieces_by_replica(w)
        return dict(ci=ci, rows_t=rows_t, cb=cbs)

    def acol_pair(p, k):
        pair = slice(k * CHUNK, (k + 2) * CHUNK)
        pair_b = slice(bwd0 + k * CHUNK, bwd0 + (k + 2) * CHUNK)
        a_pieces = pieces[p["ci"], 0]
        return (jnp.dot(a_pieces, ecol_ref[:, pair], preferred_element_type=F32),
                jnp.dot(a_pieces, ecol_ref[:, pair_b], preferred_element_type=F32))

    def m_pair(p, k, acol):
        rows_t, cb = p["rows_t"], p["cb"][k // heads_per_group]
        ms = []
        for hh in range(2):
            kk = k + hh
            cols = slice(hh * CHUNK, (hh + 1) * CHUNK)
            arg_f = acol[0][:, cols] - rows_t[kk:kk + 1, :]
            arg_b = acol[1][:, cols] - rows_t[16 + kk:17 + kk, :]
            arg = jnp.where(below, arg_f, jnp.where(above, arg_b, rows_t[32 + kk:33 + kk, :]))
            ms.append((jnp.exp2(arg) * cb).astype(BF16))
        return ms

    def diag_block(ci, k0, ms):
        rows = _chunk_rows(ci)
        qs = slice(k0 * SSD_HEAD_DIM, k0 * SSD_HEAD_DIM + wide)
        xq = xbc_ref[rows, qs]
        lhs = jnp.concatenate(ms + [eye2_ref[...]], axis=1)
        rhs = jnp.concatenate([xq * headmask_ref[hh:hh + 1, :] for hh in range(DIAG_HEADS)]
                              + [xq * dskip[:, qs], yb_ref[rows, qs]], axis=0)
        return jnp.dot(lhs, rhs, preferred_element_type=F32)

    def carried(ci, p, diag):
        rows = _chunk_rows(ci)
        per_group = heads_per_group // DIAG_HEADS
        for g in range(SSD_GROUPS):
            hs = slice(g * half, (g + 1) * half)
            b_g = xbc_ref[rows, D_SSD + g * SSD_STATE:D_SSD + (g + 1) * SSD_STATE]
            c_g = xbc_ref[rows, D_SSD + D_BC + g * SSD_STATE:D_SSD + D_BC + (g + 1) * SSD_STATE]
            y_diag = jnp.concatenate(diag[g * per_group:(g + 1) * per_group], axis=1)
            zf = z_ref[rows, hs].astype(F32)
            gate = zf * (1.0 / (1.0 + jnp.exp(-zf)))
            ea_exp = jnp.dot(pieces[ci, 1], e64_ref[:, hs], preferred_element_type=F32)
            w_exp = jnp.dot(pieces[ci, 2], e64_ref[:, hs], preferred_element_type=F32)
            y_off = jnp.dot(c_g, state[g].astype(BF16), preferred_element_type=F32) * ea_exp
            o_ref[rows, hs] = _rms((y_diag + y_off) * gate, ng_ref[:, hs]).astype(BF16)
            _state_step(state, g, b_g, xbc_ref[rows, hs], w_exp, ea_exp[CHUNK - 1:CHUNK, :])

    preps = [prep(ci) for ci in range(SSD_STEP_CHUNKS)]
    items = [(ci, k) for ci in range(SSD_STEP_CHUNKS) for k in range(0, N_SSD_HEADS, 2)]
    acols, ms, diag = {}, [], []
    for n in range(len(items) + FWD_LOOKAHEAD):
        if n < len(items):
            ci, k = items[n]
            acols[n] = acol_pair(preps[ci], k)
        m = n - FWD_LOOKAHEAD
        if m >= 0:
            ci, k = items[m]
            ms.extend(m_pair(preps[ci], k, acols.pop(m)))
            if len(ms) == DIAG_HEADS:
                diag.append(diag_block(ci, k + 2 - DIAG_HEADS, ms))
                ms = []
            if k + 2 == N_SSD_HEADS:
                carried(ci, preps[ci], diag)
                diag = []


def _ssd(xbc, dt, z, conv_w, conv_b, dt_bias, a_log, dskip, norm_g, consts, batch, seq):
    t = xbc.shape[0]
    nblk = seq // SSD_BLOCK
    per = SSD_BLOCK // HALO
    last = t // HALO - 1
    tri, ecol, e64, shift, eye2, headmask = (
        consts[n] for n in ("tri", "ecol", "e64", "shift", "eye2", "headmask"))
    fwd = lambda b, c: (b * nblk + c, 0)
    bwd = lambda b, c: (b * nblk + (nblk - 1 - c), 0)
    halo_prev = lambda b, c: (jnp.maximum((b * nblk + (nblk - 1 - c)) * per - 1, 0), 0)
    halo_next = lambda b, c: (jnp.minimum((b * nblk + (nblk - c)) * per, last), 0)
    state = pltpu.VMEM((SSD_GROUPS, SSD_STATE, D_SSD // SSD_GROUPS), F32)
    xbcs, yb = pl.pallas_call(
        functools.partial(_ssd_bwd_kernel, nblk),
        grid=(batch, nblk),
        in_specs=[pl.BlockSpec((HALO, D_XBC), halo_prev), pl.BlockSpec((SSD_BLOCK, D_XBC), bwd),
                  pl.BlockSpec((HALO, D_XBC), halo_next), pl.BlockSpec((SSD_BLOCK, LANES), bwd),
                  _const_spec(conv_w.shape), _const_spec(conv_b.shape),
                  _const_spec(dt_bias.shape), _const_spec(a_log.shape),
                  _const_spec(tri.shape), _const_spec(e64.shape), _const_spec(shift.shape)],
        out_specs=[pl.BlockSpec((SSD_BLOCK, D_XBC), bwd), pl.BlockSpec((SSD_BLOCK, D_SSD), bwd)],
        out_shape=[jax.ShapeDtypeStruct((t, D_XBC), BF16), jax.ShapeDtypeStruct((t, D_SSD), BF16)],
        scratch_shapes=[state, pltpu.VMEM((SSD_BLOCK + CHUNK, D_XBC), BF16)],
        compiler_params=_params(("parallel", "arbitrary")),
        name="ssd_bwd",
    )(xbc, xbc, xbc, dt, conv_w, conv_b, dt_bias, a_log, tri, e64, shift)
    return pl.pallas_call(
        _ssd_fwd_kernel,
        grid=(batch, nblk),
        in_specs=[pl.BlockSpec((SSD_BLOCK, D_XBC), fwd), pl.BlockSpec((SSD_BLOCK, LANES), fwd),
                  pl.BlockSpec((SSD_BLOCK, D_SSD), fwd), pl.BlockSpec((SSD_BLOCK, D_SSD), fwd),
                  _const_spec(dt_bias.shape), _const_spec(a_log.shape),
                  _const_spec(dskip.shape), _const_spec(norm_g.shape),
                  _const_spec(tri.shape), _const_spec(ecol.shape), _const_spec(e64.shape),
                  _const_spec(eye2.shape), _const_spec(headmask.shape)],
        out_specs=pl.BlockSpec((SSD_BLOCK, D_SSD), fwd),
        out_shape=jax.ShapeDtypeStruct((t, D_SSD), BF16),
        scratch_shapes=[state, pltpu.VMEM((SSD_STEP_CHUNKS, 3, CHUNK, LANES), BF16)],
        compiler_params=_params(("parallel", "arbitrary")),
        name="ssd_fwd",
    )(xbcs, dt, z, yb, dt_bias, a_log, dskip, norm_g, tri, ecol, e64, eye2, headmask)


def _outproj_kernel(x_ref, att_ref, ssd_ref, wa_ref, ws_ref, gpost_ref, gpre_ref, x1_ref, h_ref):
    mix = (jnp.dot(att_ref[...], wa_ref[...], preferred_element_type=F32)
           + jnp.dot(ssd_ref[...], ws_ref[...], preferred_element_type=F32))
    x1 = x_ref[...] + _rms(mix, gpost_ref[...])
    x1_ref[...] = x1
    h_ref[...] = _rms(x1, gpre_ref[...]).astype(BF16)


def _outproj(x2, att, ssd, wa, ws, gpost, gpre):
    t = x2.shape[0]
    row = lambda i: (i, 0)
    return pl.pallas_call(
        _outproj_kernel,
        grid=(t // TOK_BLOCK,),
        in_specs=[pl.BlockSpec((TOK_BLOCK, D_MODEL), row), pl.BlockSpec((TOK_BLOCK, D_ATTN), row),
                  pl.BlockSpec((TOK_BLOCK, D_SSD), row), _const_spec(wa.shape), _const_spec(ws.shape),
                  _const_spec(gpost.shape), _const_spec(gpre.shape)],
        out_specs=[pl.BlockSpec((TOK_BLOCK, D_MODEL), row), pl.BlockSpec((TOK_BLOCK, D_MODEL), row)],
        out_shape=[jax.ShapeDtypeStruct((t, D_MODEL), F32), jax.ShapeDtypeStruct((t, D_MODEL), BF16)],
        compiler_params=_params(("parallel",)),
        name="outproj",
    )(x2, att, ssd, wa, ws, gpost, gpre)


def _ffn_kernel(nblk, hp_ref, hc_ref, hn_ref, x1_ref, wg_ref, wu_ref, wd_ref, cw_ref, cb_ref,
                gpost_ref, o_ref, hext, gbuf, act):
    i = pl.program_id(0) % nblk
    hext[0:HALO, :] = hp_ref[...]
    hext[HALO:HALO + TOK_BLOCK, :] = hc_ref[...]
    hext[HALO + TOK_BLOCK:, :] = hn_ref[...]
    keep_prev = jnp.where(i == 0, 0.0, 1.0)
    keep_next = jnp.where(i == nblk - 1, 0.0, 1.0)
    for c in range(D_FF // FF_COLS):
        cols = slice(c * FF_COLS, (c + 1) * FF_COLS)
        gbuf[...] = jnp.dot(hext[...], wg_ref[:, cols], preferred_element_type=F32)
        up = jnp.dot(hc_ref[...], wu_ref[:, cols], preferred_element_type=F32)
        gp = gbuf[HALO - 1:HALO - 1 + TOK_BLOCK, :]
        gc = gbuf[HALO:HALO + TOK_BLOCK, :]
        gn = gbuf[HALO + 1:HALO + 1 + TOK_BLOCK, :]
        row = lax.broadcasted_iota(jnp.int32, (TOK_BLOCK, FF_COLS), 0)
        gp = jnp.where(row == 0, gp * keep_prev, gp)
        gn = jnp.where(row == TOK_BLOCK - 1, gn * keep_next, gn)
        gate = (gp * cw_ref[0:1, cols] + gc * cw_ref[1:2, cols] + gn * cw_ref[2:3, cols]
                + cb_ref[:, cols])
        gelu = 0.5 * gate * (1.0 + jnp.tanh(0.7978845608028654 * (gate + 0.044715 * gate * gate * gate)))
        act[:, cols] = (gelu * up).astype(BF16)
    f = jnp.dot(act[...], wd_ref[...], preferred_element_type=F32)
    o_ref[...] = x1_ref[...] + _rms(f, gpost_ref[...])


def _ffn(h, x1, wg, wu, wd, cw, cb, gpost, seq):
    t = h.shape[0]
    nblk = seq // TOK_BLOCK
    per = TOK_BLOCK // HALO
    last = t // HALO - 1
    row = lambda i: (i, 0)
    return pl.pallas_call(
        functools.partial(_ffn_kernel, nblk),
        grid=(t // TOK_BLOCK,),
        in_specs=[pl.BlockSpec((HALO, D_MODEL), lambda i: (jnp.maximum(i * per - 1, 0), 0)),
                  pl.BlockSpec((TOK_BLOCK, D_MODEL), row),
                  pl.BlockSpec((HALO, D_MODEL), lambda i: (jnp.minimum((i + 1) * per, last), 0)),
                  pl.BlockSpec((TOK_BLOCK, D_MODEL), row),
                  _const_spec(wg.shape), _const_spec(wu.shape), _const_spec(wd.shape),
                  _const_spec(cw.shape), _const_spec(cb.shape), _const_spec(gpost.shape)],
        out_specs=pl.BlockSpec((TOK_BLOCK, D_MODEL), row),
        out_shape=jax.ShapeDtypeStruct((t, D_MODEL), F32),
        scratch_shapes=[pltpu.VMEM((TOK_BLOCK + 2 * HALO, D_MODEL), BF16),
                        pltpu.VMEM((TOK_BLOCK + 2 * HALO, FF_COLS), F32),
                        pltpu.VMEM((TOK_BLOCK, D_FF), BF16)],
        compiler_params=_params(("parallel",)),
        name="ffn",
    )(h, h, h, x1, wg, wu, wd, cw, cb, gpost)


def _resident_spec(shape):
    nd = len(shape)
    return pl.BlockSpec(shape, lambda *_: (0,) * nd, pipeline_mode=pl.Buffered(1))


def _mix_ffn_kernel(nblk, xp_ref, xc_ref, xn_ref, ap_ref, ac_ref, an_ref, sp_ref, sc_ref, sn_ref,
                    wa_ref, ws_ref, gpost_ref, gpre_ref, wg_ref, wu_ref, wd_ref, cw_ref, cb_ref,
                    gffn_ref, o_ref, xext, aext, sext, hext, gbuf, act):
    i = pl.program_id(0) % nblk
    main = slice(HALO, HALO + TOK_BLOCK)
    for ext, (p_ref, c_ref, n_ref) in ((xext, (xp_ref, xc_ref, xn_ref)), (aext, (ap_ref, ac_ref, an_ref)),
                                       (sext, (sp_ref, sc_ref, sn_ref))):
        ext[0:HALO, :] = p_ref[...]
        ext[main, :] = c_ref[...]
        ext[HALO + TOK_BLOCK:, :] = n_ref[...]
    mix = (jnp.dot(aext[...], wa_ref[...], preferred_element_type=F32)
           + jnp.dot(sext[...], ws_ref[...], preferred_element_type=F32))
    x1 = xext[...] + _rms(mix, gpost_ref[...])
    xext[...] = x1
    hext[...] = _rms(x1, gpre_ref[...]).astype(BF16)
    keep_prev = jnp.where(i == 0, 0.0, 1.0)
    keep_next = jnp.where(i == nblk - 1, 0.0, 1.0)
    row = lax.broadcasted_iota(jnp.int32, (TOK_BLOCK, FF_COLS), 0)
    for c in range(D_FF // FF_COLS):
        cols = slice(c * FF_COLS, (c + 1) * FF_COLS)
        gbuf[...] = jnp.dot(hext[...], wg_ref[:, cols], preferred_element_type=F32)
        up = jnp.dot(hext[main, :], wu_ref[:, cols], preferred_element_type=F32)
        gp = gbuf[HALO - 1:HALO - 1 + TOK_BLOCK, :]
        gc = gbuf[main, :]
        gn = gbuf[HALO + 1:HALO + 1 + TOK_BLOCK, :]
        gp = jnp.where(row == 0, gp * keep_prev, gp)
        gn = jnp.where(row == TOK_BLOCK - 1, gn * keep_next, gn)
        gate = (gp * cw_ref[0:1, cols] + gc * cw_ref[1:2, cols] + gn * cw_ref[2:3, cols]
                + cb_ref[:, cols])
        gelu = 0.5 * gate * (1.0 + jnp.tanh(0.7978845608028654 * (gate + 0.044715 * gate * gate * gate)))
        act[:, cols] = (gelu * up).astype(BF16)
    f = jnp.dot(act[...], wd_ref[...], preferred_element_type=F32)
    o_ref[...] = xext[main, :] + _rms(f, gffn_ref[...])


def _mix_ffn(x2, att, ssd, p, seq):
    t = x2.shape[0]
    nblk = seq // TOK_BLOCK
    per = TOK_BLOCK // HALO
    last = t // HALO - 1
    row = lambda i: (i, 0)
    prev = lambda i: (jnp.maximum(i * per - 1, 0), 0)
    nxt = lambda i: (jnp.minimum((i + 1) * per, last), 0)

    def halo_specs(width):
        return [pl.BlockSpec((HALO, width), prev), pl.BlockSpec((TOK_BLOCK, width), row),
                pl.BlockSpec((HALO, width), nxt)]

    weights = [p[n] for n in ("wo_a", "wo_s", "g_mix_post", "g_ffn_pre", "wg", "wu", "wd",
                              "ffn_cw", "ffn_cb", "g_ffn_post")]
    ext_rows = TOK_BLOCK + 2 * HALO
    return pl.pallas_call(
        functools.partial(_mix_ffn_kernel, nblk),
        grid=(t // TOK_BLOCK,),
        in_specs=(halo_specs(D_MODEL) + halo_specs(D_ATTN) + halo_specs(D_SSD)
                  + [_resident_spec(w.shape) for w in weights]),
        out_specs=pl.BlockSpec((TOK_BLOCK, D_MODEL), row),
        out_shape=jax.ShapeDtypeStruct((t, D_MODEL), F32),
        scratch_shapes=[pltpu.VMEM((ext_rows, D_MODEL), F32), pltpu.VMEM((ext_rows, D_ATTN), BF16),
                        pltpu.VMEM((ext_rows, D_SSD), BF16), pltpu.VMEM((ext_rows, D_MODEL), BF16),
                        pltpu.VMEM((ext_rows, FF_COLS), F32), pltpu.VMEM((TOK_BLOCK, D_FF), BF16)],
        compiler_params=_params(("parallel",)),
        name="mix_ffn",
    )(x2, x2, x2, att, att, att, ssd, ssd, ssd, *weights)


def _dt_tile(v):
    return jnp.tile(v, (1,) * (v.ndim - 1) + (DT_REP,))


def _prepare(norm_mix_pre, norm_mix_post, w_in, attn_rpb, attn_out_norm, ssd_conv_w, ssd_conv_b,
             ssd_dt_bias, ssd_a_log, ssd_d, ssd_norm, w_out, norm_ffn_pre, norm_ffn_post,
             w_gate, w_up, ffn_conv_w, ffn_conv_b, w_down):
    row = lambda v: v.reshape(1, -1).astype(F32)
    o_z = 3 * D_ATTN
    o_x = o_z + D_SSD
    o_dt = o_x + D_XBC
    scale = ATTN_HEAD_DIM ** -0.5
    wqkv = jnp.concatenate([w_in[:, :D_ATTN] * scale, w_in[:, D_ATTN:o_z]], axis=1)
    return dict(
        g_mix_pre=row(norm_mix_pre), g_mix_post=row(norm_mix_post),
        wqkv=wqkv.astype(BF16), wz=w_in[:, o_z:o_x].astype(BF16), wxbc=w_in[:, o_x:o_dt].astype(BF16),
        wdt=_dt_tile(w_in[:, o_dt:]).astype(BF16),
        bias_tab=_attn_bias_table(attn_rpb), g_attn=row(attn_out_norm),
        conv_w=ssd_conv_w.astype(F32), conv_b=row(ssd_conv_b),
        dt_bias=_dt_tile(row(ssd_dt_bias)), a_log=_dt_tile(row(ssd_a_log)),
        dskip=row(jnp.repeat(ssd_d, SSD_HEAD_DIM)), g_ssd=row(ssd_norm),
        wo_a=w_out[:D_ATTN].astype(BF16), wo_s=w_out[D_ATTN:].astype(BF16),
        g_ffn_pre=row(norm_ffn_pre), g_ffn_post=row(norm_ffn_post),
        wg=w_gate.astype(BF16), wu=w_up.astype(BF16), wd=w_down.astype(BF16),
        ffn_cw=ffn_conv_w.astype(F32), ffn_cb=row(ffn_conv_b),
    )


def _layer(x, p, consts):
    batch, seq, _ = x.shape
    x2 = x.reshape(batch * seq, D_MODEL)
    qkv, z, xbc, dt = _inproj(x2, p["g_mix_pre"], p["wqkv"], p["wz"], p["wxbc"], p["wdt"])
    att = _attention(qkv, p["bias_tab"], p["g_attn"], batch, seq)
    ssd = _ssd(xbc, dt, z, p["conv_w"], p["conv_b"], p["dt_bias"], p["a_log"], p["dskip"], p["g_ssd"],
               consts, batch, seq)
    y = _mix_ffn(x2, att, ssd, p, seq)
    return y.reshape(batch, seq, D_MODEL)


def kernel(x_prompt, x_sample, norm_mix_pre, norm_mix_post, w_in, attn_rpb, attn_out_norm, ssd_conv_w, ssd_conv_b, ssd_dt_bias, ssd_a_log, ssd_d, ssd_norm, w_out, norm_ffn_pre, norm_ffn_post, w_gate, w_up, ffn_conv_w, ffn_conv_b, w_down):
    assert norm_mix_pre.shape[0] == 1, "single-layer trunk"
    p = _prepare(norm_mix_pre[0], norm_mix_post[0], w_in[0], attn_rpb[0], attn_out_norm[0],
                 ssd_conv_w[0], ssd_conv_b[0], ssd_dt_bias[0], ssd_a_log[0], ssd_d[0], ssd_norm[0],
                 w_out[0], norm_ffn_pre[0], norm_ffn_post[0], w_gate[0], w_up[0],
                 ffn_conv_w[0], ffn_conv_b[0], w_down[0])
    consts = _ssd_constants()
    return (_layer(x_prompt, p, consts), _layer(x_sample, p, consts))
```

```python
import functools

import numpy as np
import jax
import jax.numpy as jnp
from jax import lax
from jax.experimental import pallas as pl
from jax.experimental.pallas import tpu as pltpu

F32 = jnp.float32
BF16 = jnp.bfloat16

D_MODEL = 1024
GRID_W = 64
N_ATTN_HEADS = 8
ATTN_HEAD_DIM = 64
D_ATTN = N_ATTN_HEADS * ATTN_HEAD_DIM
WIN_ROWS = 8
WIN_COLS = 16
D_SSD = 1024
SSD_HEAD_DIM = 64
N_SSD_HEADS = 16
SSD_GROUPS = 2
SSD_STATE = 128
SSD_CONV_W = 5
CHUNK = 128
D_BC = SSD_GROUPS * SSD_STATE
D_XBC = D_SSD + 2 * D_BC
D_FF = 2816
FFN_CONV_W = 3
EPS = 1e-6
LOG2E = 1.4426950408889634

LANES = 128
BF16_SUBLANES = 16
MXU_N = 256
VMEM_LIMIT = 56 * 1024 * 1024

TOK_BLOCK = 512
ATT_ROWS = 8
ATT_BLOCK = ATT_ROWS * GRID_W
ATT_WIN_ROWS = 3 * ATT_ROWS
HEADS_PER_TILE = LANES // ATTN_HEAD_DIM
N_HEAD_TILES = N_ATTN_HEADS // HEADS_PER_TILE
N_BIAS_TYPES = WIN_ROWS
MASK_VALUE = -1e30
HALO = BF16_SUBLANES
DT_REP = LANES // (2 * N_SSD_HEADS)
DIAG_HEADS = 4
FF_COLS = MXU_N


def _params(sem):
    return pltpu.CompilerParams(dimension_semantics=sem, vmem_limit_bytes=VMEM_LIMIT)


def _const_spec(shape):
    nd = len(shape)
    return pl.BlockSpec(shape, lambda *_: (0,) * nd)


def _resident_spec(shape):
    nd = len(shape)
    return pl.BlockSpec(shape, lambda *_: (0,) * nd, pipeline_mode=pl.Buffered(1))


def _rms(x, gain):
    return x * lax.rsqrt(jnp.mean(x * x, axis=-1, keepdims=True) + EPS) * gain


def _split3(v):
    hi = v.astype(BF16)
    r1 = v - hi.astype(F32)
    mid = r1.astype(BF16)
    lo = (r1 - mid.astype(F32)).astype(BF16)
    return hi, mid, lo


def _silu(v):
    return v * (1.0 / (1.0 + jnp.exp(-v)))


def _inproj_kernel(nblk, xp_ref, xc_ref, xn_ref, g_ref, wqkv_ref, wz_ref, wxbc_ref, wdt_ref,
                   cw_ref, cbias_ref, qkv_ref, z_ref, xbc_ref, dt_ref, hext, cbuf0, cbuf1):
    i = pl.program_id(0) % nblk
    gain = g_ref[...]
    main = slice(HALO, HALO + TOK_BLOCK)
    hext[0:HALO, :] = (_rms(xp_ref[...], gain) * jnp.where(i == 0, 0.0, 1.0)).astype(BF16)
    hext[main, :] = _rms(xc_ref[...], gain).astype(BF16)
    hext[HALO + TOK_BLOCK:, :] = (_rms(xn_ref[...], gain) * jnp.where(i == nblk - 1, 0.0, 1.0)).astype(BF16)
    pad = SSD_CONV_W // 2

    def xbc_cols(c, buf):
        buf[...] = jnp.dot(hext[...], wxbc_ref[:, c * MXU_N:(c + 1) * MXU_N], preferred_element_type=F32)

    def conv_cols(c, buf):
        cols = slice(c * MXU_N, (c + 1) * MXU_N)
        acc = cbias_ref[:, cols] + buf[HALO - pad:HALO - pad + TOK_BLOCK, :] * cw_ref[0:1, cols]
        for tap in range(1, SSD_CONV_W):
            acc = acc + buf[HALO - pad + tap:HALO - pad + tap + TOK_BLOCK, :] * cw_ref[tap:tap + 1, cols]
        xbc_ref[:, cols] = _silu(acc).astype(BF16)

    def qkv_cols(c):
        cols = slice(c * MXU_N, (c + 1) * MXU_N)
        qkv_ref[:, cols] = jnp.dot(hext[main, :], wqkv_ref[:, cols], preferred_element_type=F32).astype(BF16)

    def z_cols(c):
        cols = slice(c * MXU_N, (c + 1) * MXU_N)
        z_ref[:, cols] = _silu(jnp.dot(hext[main, :], wz_ref[:, cols], preferred_element_type=F32)).astype(BF16)

    def dt_cols(_):
        dt_ref[...] = jnp.dot(hext[main, :], wdt_ref[...], preferred_element_type=F32)

    other = ([(qkv_cols, c) for c in range(3 * D_ATTN // MXU_N)] + [(z_cols, c) for c in range(D_SSD // MXU_N)]
             + [(dt_cols, 0)])
    bufs = (cbuf0, cbuf1)
    n_conv = D_XBC // MXU_N
    per_conv = -(-len(other) // n_conv)
    xbc_cols(0, bufs[0])
    for c in range(n_conv):
        if c + 1 < n_conv:
            xbc_cols(c + 1, bufs[(c + 1) % 2])
        for fn, arg in other[c * per_conv:(c + 1) * per_conv]:
            fn(arg)
        conv_cols(c, bufs[c % 2])


def _inproj(x2, p, seq):
    t = x2.shape[0]
    nblk = seq // TOK_BLOCK
    per = TOK_BLOCK // HALO
    last = t // HALO - 1
    row = lambda i: (i, 0)
    weights = [p[n] for n in ("g_mix_pre", "wqkv", "wz", "wxbc", "wdt", "conv_w", "conv_b")]
    ext_rows = TOK_BLOCK + 2 * HALO
    return pl.pallas_call(
        functools.partial(_inproj_kernel, nblk),
        grid=(t // TOK_BLOCK,),
        in_specs=[pl.BlockSpec((HALO, D_MODEL), lambda i: (jnp.maximum(i * per - 1, 0), 0)),
                  pl.BlockSpec((TOK_BLOCK, D_MODEL), row),
                  pl.BlockSpec((HALO, D_MODEL), lambda i: (jnp.minimum((i + 1) * per, last), 0))]
                 + [_resident_spec(w.shape) for w in weights],
        out_specs=[pl.BlockSpec((TOK_BLOCK, 3 * D_ATTN), row),
                   pl.BlockSpec((TOK_BLOCK, D_SSD), row),
                   pl.BlockSpec((TOK_BLOCK, D_XBC), row),
                   pl.BlockSpec((TOK_BLOCK, LANES), row)],
        out_shape=[jax.ShapeDtypeStruct((t, 3 * D_ATTN), BF16),
                   jax.ShapeDtypeStruct((t, D_SSD), BF16),
                   jax.ShapeDtypeStruct((t, D_XBC), BF16),
                   jax.ShapeDtypeStruct((t, LANES), F32)],
        scratch_shapes=[pltpu.VMEM((ext_rows, D_MODEL), BF16),
                        pltpu.VMEM((ext_rows, MXU_N), F32), pltpu.VMEM((ext_rows, MXU_N), F32)],
        compiler_params=_params(("parallel",)),
        name="inproj",
    )(x2, x2, x2, *weights)


def _attn_window_base(j, rows):
    return jnp.clip((j - 1) * ATT_ROWS, 0, rows - ATT_WIN_ROWS)


def _attn_kernel(rows, q_ref, kwin, vwin, bias_ref, g_ref, o_ref):
    j = pl.program_id(1)
    base = _attn_window_base(j, rows)
    lane = lax.broadcasted_iota(jnp.int32, (GRID_W, LANES), 1)
    first_head = lane < ATTN_HEAD_DIM
    gain = g_ref[...]
    ones = jnp.ones((WIN_ROWS * GRID_W, LANES), BF16)
    n_keys = WIN_ROWS * GRID_W

    def window(i):
        r = j * ATT_ROWS + i
        rs = jnp.clip(r - WIN_ROWS // 2, 0, rows - WIN_ROWS)
        start = pl.multiple_of((rs - base) * GRID_W, GRID_W)
        return start, rs - r + (WIN_ROWS - 1)

    def scores(i, hp, start, btype):
        cols = slice(hp * LANES, (hp + 1) * LANES)
        q2 = q_ref[i * GRID_W:(i + 1) * GRID_W, cols]
        zero = jnp.zeros_like(q2)
        qbd = jnp.concatenate([jnp.where(first_head, q2, zero),
                               jnp.where(first_head, zero, q2)], axis=0)
        kp = kwin[pl.ds(start, n_keys), cols]
        s = lax.dot_general(qbd, kp, (((1,), (1,)), ((), ())), preferred_element_type=F32)
        bias = jnp.concatenate([bias_ref[hp, btype + 2 * t] for t in range(WIN_ROWS // 2)], axis=1)
        return s + bias

    def probs(s):
        return jnp.exp(s - jnp.max(s, axis=-1, keepdims=True)).astype(BF16)

    def weighted(hp, start, p):
        cols = slice(hp * LANES, (hp + 1) * LANES)
        v1 = jnp.concatenate([vwin[pl.ds(start, n_keys), cols], ones], axis=1)
        o = jnp.dot(p, v1, preferred_element_type=F32)
        o = o[:, :LANES] / o[:, LANES:]
        return jnp.where(first_head, o[:GRID_W], o[GRID_W:])

    items = [(i, hp) for i in range(ATT_ROWS) for hp in range(N_HEAD_TILES)]
    wins = [window(i) for i in range(ATT_ROWS)]
    s_val, p_val, outs = {}, {}, {}
    for n in range(len(items) + 2):
        if n < len(items):
            i, hp = items[n]
            s_val[n] = scores(i, hp, *wins[i])
        if 1 <= n <= len(items):
            p_val[n - 1] = probs(s_val.pop(n - 1))
        if n >= 2:
            i, hp = items[n - 2]
            outs[hp] = weighted(hp, wins[i][0], p_val.pop(n - 2))
            if hp == N_HEAD_TILES - 1:
                att = jnp.concatenate([outs[h] for h in range(N_HEAD_TILES)], axis=1)
                o_ref[i * GRID_W:(i + 1) * GRID_W, :] = _rms(att, gain).astype(BF16)


def _attention(qkv, bias_tab, gain, batch, seq):
    t = qkv.shape[0]
    rows = seq // GRID_W
    nblk = seq // ATT_BLOCK

    assert rows >= ATT_WIN_ROWS

    def kv_spec(col):
        return pl.BlockSpec(
            (pl.Element(ATT_WIN_ROWS * GRID_W), pl.Element(D_ATTN)),
            lambda b, j: (pl.multiple_of(b * seq + _attn_window_base(j, rows) * GRID_W, GRID_W),
                          col * D_ATTN))

    return pl.pallas_call(
        functools.partial(_attn_kernel, rows),
        grid=(batch, nblk),
        in_specs=[pl.BlockSpec((ATT_BLOCK, D_ATTN), lambda b, j: (b * nblk + j, 0)),
                  kv_spec(1), kv_spec(2),
                  _const_spec(bias_tab.shape), _const_spec(gain.shape)],
        out_specs=pl.BlockSpec((ATT_BLOCK, D_ATTN), lambda b, j: (b * nblk + j, 0)),
        out_shape=jax.ShapeDtypeStruct((t, D_ATTN), BF16),
        compiler_params=_params(("parallel", "arbitrary")),
        name="natten",
    )(qkv, qkv, qkv, bias_tab, gain)


def _attn_bias_table(rpb):
    pad = GRID_W - WIN_COLS
    padded = jnp.pad(rpb.astype(F32), ((0, 0), (0, 0), (pad, pad)))
    first = WIN_COLS - 1 + pad
    toeplitz = jnp.stack([padded[:, :, first - q:first - q + GRID_W] for q in range(GRID_W)], axis=2)
    qc = np.arange(GRID_W)[:, None]
    kc = np.arange(GRID_W)[None, :]
    cs = np.clip(qc - WIN_COLS // 2, 0, GRID_W - WIN_COLS)
    valid = (kc >= cs) & (kc < cs + WIN_COLS)
    toeplitz = jnp.where(jnp.asarray(valid), toeplitz, MASK_VALUE)
    pair = jnp.concatenate([toeplitz[:, :-1], toeplitz[:, 1:]], axis=-1)
    n_off = pair.shape[1]
    pair = pair.reshape(N_HEAD_TILES, HEADS_PER_TILE, n_off, GRID_W, LANES).transpose(0, 2, 1, 3, 4)
    return pair.reshape(N_HEAD_TILES, n_off, HEADS_PER_TILE * GRID_W, LANES)


SSD_STEP_CHUNKS = 8
SSD_BLOCK = SSD_STEP_CHUNKS * CHUNK
FWD_LOOKAHEAD = 4


def _ssd_constants():
    lane = np.arange(LANES)
    m, d, k = lane // 32, (lane // 16) % 2, lane % 16
    tri_l = (np.arange(CHUNK)[:, None] >= np.arange(CHUNK)[None, :]).astype(np.float32)
    tri = np.concatenate([np.tile(tri_l, (1, 3)), np.tile(tri_l.T, (1, 3))], axis=0)
    ecol = np.zeros((LANES, 2 * N_SSD_HEADS * CHUNK), np.float32)
    e64 = np.zeros((LANES, 2 * D_SSD), np.float32)
    for ln in range(LANES):
        if m[ln] < 3:
            c0 = (d[ln] * N_SSD_HEADS + k[ln]) * CHUNK
            ecol[ln, c0:c0 + CHUNK] = 1.0
            c1 = d[ln] * D_SSD + k[ln] * SSD_HEAD_DIM
            e64[ln, c1:c1 + SSD_HEAD_DIM] = 1.0
    eye2 = np.concatenate([np.eye(CHUNK), np.eye(CHUNK)], axis=1).astype(np.float32)
    headmask = (np.arange(DIAG_HEADS)[:, None] == np.arange(DIAG_HEADS * SSD_HEAD_DIM)[None, :] // SSD_HEAD_DIM)
    return dict(tri=jnp.asarray(tri, BF16), ecol=jnp.asarray(ecol, BF16), e64=jnp.asarray(e64, BF16),
                eye2=jnp.asarray(eye2, BF16), headmask=jnp.asarray(headmask.astype(np.float32), BF16))


def _pieces_by_replica(v):
    hi, mid, lo = _split3(v)
    rep = lax.broadcasted_iota(jnp.int32, v.shape, 1) >> 5
    zero = jnp.zeros_like(hi)
    return jnp.where(rep == 0, hi, jnp.where(rep == 1, mid, jnp.where(rep == 2, lo, zero)))


def _chunk_rows(ci):
    return slice(ci * CHUNK, (ci + 1) * CHUNK)


def _ssd_dt_prep(ci, dt_ref, dtb_ref, alog_ref, tri_ref):
    x = dt_ref[_chunk_rows(ci), :] + dtb_ref[...]
    dt = jnp.maximum(x, 0.0) + jnp.log(1.0 + jnp.exp(-jnp.abs(x)))
    da = dt * (-LOG2E * jnp.exp(alog_ref[...]))
    hi, mid, lo = _split3(da)
    cs = jnp.dot(tri_ref[...], jnp.concatenate([hi, mid, lo], axis=0), preferred_element_type=F32)
    bwd_lane = ((lax.broadcasted_iota(jnp.int32, (CHUNK, LANES), 1) >> 4) & 1) == 1
    a = jnp.where(bwd_lane, cs[CHUNK:], cs[:CHUNK])
    return dt, a, bwd_lane


def _chunk_totals(a, bwd_lane):
    return jnp.where(bwd_lane[:1], a[:1], a[CHUNK - 1:])


def _state_step(state_ref, g, b_g, x_g, w_exp_g, decay_row):
    xw = x_g * w_exp_g.astype(BF16)
    bt = b_g.astype(F32).T.astype(BF16)
    state_ref[g] = state_ref[g] * decay_row + jnp.dot(bt, xw, preferred_element_type=F32)


def _ssd_bwd_kernel(xbc_ref, dt_ref, dtb_ref, alog_ref, tri_ref, e64_ref, yb_ref, state):
    @pl.when(pl.program_id(1) == 0)
    def _():
        state[...] = jnp.zeros_like(state)

    half = D_SSD // SSD_GROUPS

    def prep(ci):
        dt, a, bwd_lane = _ssd_dt_prep(ci, dt_ref, dtb_ref, alog_ref, tri_ref)
        w = jnp.exp2(_chunk_totals(a, bwd_lane) - a) * dt
        return _pieces_by_replica(jnp.exp2(a)), _pieces_by_replica(w)

    def state_part(ci, ea_pieces, w_pieces):
        rows = _chunk_rows(ci)
        for g in range(SSD_GROUPS):
            hs = slice(g * half, (g + 1) * half)
            bs = slice(D_SSD + g * half, D_SSD + (g + 1) * half)
            b_g = xbc_ref[rows, D_SSD + g * SSD_STATE:D_SSD + (g + 1) * SSD_STATE]
            c_g = xbc_ref[rows, D_SSD + D_BC + g * SSD_STATE:D_SSD + D_BC + (g + 1) * SSD_STATE]
            ea_exp = jnp.dot(ea_pieces, e64_ref[:, bs], preferred_element_type=F32)
            w_exp = jnp.dot(w_pieces, e64_ref[:, bs], preferred_element_type=F32)
            y = jnp.dot(c_g, state[g].astype(BF16), preferred_element_type=F32)
            yb_ref[rows, hs] = (y * ea_exp).astype(BF16)
            _state_step(state, g, b_g, xbc_ref[rows, hs], w_exp, ea_exp[0:1, :])

    order = list(reversed(range(SSD_STEP_CHUNKS)))
    preps = {ci: prep(ci) for ci in order}
    for ci in order:
        state_part(ci, *preps.pop(ci))


def _ssd_fwd_kernel(xbc_ref, dt_ref, z_ref, yb_ref, dtb_ref, alog_ref, dskip_ref, ng_ref,
                    tri_ref, ecol_ref, e64_ref, eye2_ref, headmask_ref, o_ref, state, pieces):
    @pl.when(pl.program_id(1) == 0)
    def _():
        state[...] = jnp.zeros_like(state)

    dskip = dskip_ref[...].astype(BF16)
    lane = lax.broadcasted_iota(jnp.int32, (CHUNK, LANES), 1)
    li = lax.broadcasted_iota(jnp.int32, (CHUNK, CHUNK), 0)
    si = lax.broadcasted_iota(jnp.int32, (CHUNK, CHUNK), 1)
    below = si < li
    above = si > li
    half = D_SSD // SSD_GROUPS
    heads_per_group = N_SSD_HEADS // SSD_GROUPS
    wide = DIAG_HEADS * SSD_HEAD_DIM
    bwd0 = N_SSD_HEADS * CHUNK

    def prep(ci):
        dt, a, bwd_lane = _ssd_dt_prep(ci, dt_ref, dtb_ref, alog_ref, tri_ref)
        gq = a - LOG2E * jnp.log(dt)
        dt_sum = dt + pltpu.roll(dt, 16, axis=1)
        rows_t = jnp.where(lane < 32, gq, LOG2E * jnp.log(dt_sum)).T
        w = jnp.exp2(_chunk_totals(a, bwd_lane) - gq)
        rows = _chunk_rows(ci)
        cbs = []
        for g in range(SSD_GROUPS):
            b_g = xbc_ref[rows, D_SSD + g * SSD_STATE:D_SSD + (g + 1) * SSD_STATE]
            c_g = xbc_ref[rows, D_SSD + D_BC + g * SSD_STATE:D_SSD + D_BC + (g + 1) * SSD_STATE]
            cbs.append(lax.dot_general(c_g, b_g, (((1,), (1,)), ((), ())), preferred_element_type=F32))
        pieces[ci, 0] = _pieces_by_replica(a)
        pieces[ci, 1] = _pieces_by_replica(jnp.exp2(a))
        pieces[ci, 2] = _pieces_by_replica(w)
        return dict(ci=ci, rows_t=rows_t, cb=cbs)

    def acol_pair(p, k):
        pair = slice(k * CHUNK, (k + 2) * CHUNK)
        pair_b = slice(bwd0 + k * CHUNK, bwd0 + (k + 2) * CHUNK)
        a_pieces = pieces[p["ci"], 0]
        return (jnp.dot(a_pieces, ecol_ref[:, pair], preferred_element_type=F32),
                jnp.dot(a_pieces, ecol_ref[:, pair_b], preferred_element_type=F32))

    def m_pair(p, k, acol):
        rows_t, cb = p["rows_t"], p["cb"][k // heads_per_group]
        ms = []
        for hh in range(2):
            kk = k + hh
            cols = slice(hh * CHUNK, (hh + 1) * CHUNK)
            arg_f = acol[0][:, cols] - rows_t[kk:kk + 1, :]
            arg_b = acol[1][:, cols] - rows_t[16 + kk:17 + kk, :]
            arg = jnp.where(below, arg_f, jnp.where(above, arg_b, rows_t[32 + kk:33 + kk, :]))
            ms.append((jnp.exp2(arg) * cb).astype(BF16))
        return ms

    def diag_block(ci, k0, ms):
        rows = _chunk_rows(ci)
        qs = slice(k0 * SSD_HEAD_DIM, k0 * SSD_HEAD_DIM + wide)
        xq = xbc_ref[rows, qs]
        lhs = jnp.concatenate(ms + [eye2_ref[...]], axis=1)
        rhs = jnp.concatenate([xq * headmask_ref[hh:hh + 1, :] for hh in range(DIAG_HEADS)]
                              + [xq * dskip[:, qs], yb_ref[rows, qs]], axis=0)
        return jnp.dot(lhs, rhs, preferred_element_type=F32)

    def carried(ci, p, diag):
        rows = _chunk_rows(ci)
        per_group = heads_per_group // DIAG_HEADS
        for g in range(SSD_GROUPS):
            hs = slice(g * half, (g + 1) * half)
            b_g = xbc_ref[rows, D_SSD + g * SSD_STATE:D_SSD + (g + 1) * SSD_STATE]
            c_g = xbc_ref[rows, D_SSD + D_BC + g * SSD_STATE:D_SSD + D_BC + (g + 1) * SSD_STATE]
            y_diag = jnp.concatenate(diag[g * per_group:(g + 1) * per_group], axis=1)
            gate = z_ref[rows, hs].astype(F32)
            ea_exp = jnp.dot(pieces[ci, 1], e64_ref[:, hs], preferred_element_type=F32)
            w_exp = jnp.dot(pieces[ci, 2], e64_ref[:, hs], preferred_element_type=F32)
            y_off = jnp.dot(c_g, state[g].astype(BF16), preferred_element_type=F32) * ea_exp
            o_ref[rows, hs] = _rms((y_diag + y_off) * gate, ng_ref[:, hs]).astype(BF16)
            _state_step(state, g, b_g, xbc_ref[rows, hs], w_exp, ea_exp[CHUNK - 1:CHUNK, :])

    preps = [prep(ci) for ci in range(SSD_STEP_CHUNKS)]
    items = [(ci, k) for ci in range(SSD_STEP_CHUNKS) for k in range(0, N_SSD_HEADS, 2)]
    acols, ms, diag = {}, [], []
    for n in range(len(items) + FWD_LOOKAHEAD):
        if n < len(items):
            ci, k = items[n]
            acols[n] = acol_pair(preps[ci], k)
        m = n - FWD_LOOKAHEAD
        if m >= 0:
            ci, k = items[m]
            ms.extend(m_pair(preps[ci], k, acols.pop(m)))
            if len(ms) == DIAG_HEADS:
                diag.append(diag_block(ci, k + 2 - DIAG_HEADS, ms))
                ms = []
            if k + 2 == N_SSD_HEADS:
                carried(ci, preps[ci], diag)
                diag = []


def _ssd(xbcs, dt, gate, dt_bias, a_log, dskip, norm_g, consts, batch, seq):
    t = xbcs.shape[0]
    nblk = seq // SSD_BLOCK
    tri, ecol, e64, eye2, headmask = (consts[n] for n in ("tri", "ecol", "e64", "eye2", "headmask"))
    fwd = lambda b, c: (b * nblk + c, 0)
    bwd = lambda b, c: (b * nblk + (nblk - 1 - c), 0)
    state = pltpu.VMEM((SSD_GROUPS, SSD_STATE, D_SSD // SSD_GROUPS), F32)
    yb = pl.pallas_call(
        _ssd_bwd_kernel,
        grid=(batch, nblk),
        in_specs=[pl.BlockSpec((SSD_BLOCK, D_XBC), bwd), pl.BlockSpec((SSD_BLOCK, LANES), bwd),
                  _const_spec(dt_bias.shape), _const_spec(a_log.shape),
                  _const_spec(tri.shape), _const_spec(e64.shape)],
        out_specs=pl.BlockSpec((SSD_BLOCK, D_SSD), bwd),
        out_shape=jax.ShapeDtypeStruct((t, D_SSD), BF16),
        scratch_shapes=[state],
        compiler_params=_params(("parallel", "arbitrary")),
        name="ssd_bwd",
    )(xbcs, dt, dt_bias, a_log, tri, e64)
    return pl.pallas_call(
        _ssd_fwd_kernel,
        grid=(batch, nblk),
        in_specs=[pl.BlockSpec((SSD_BLOCK, D_XBC), fwd), pl.BlockSpec((SSD_BLOCK, LANES), fwd),
                  pl.BlockSpec((SSD_BLOCK, D_SSD), fwd), pl.BlockSpec((SSD_BLOCK, D_SSD), fwd),
                  _const_spec(dt_bias.shape), _const_spec(a_log.shape),
                  _const_spec(dskip.shape), _const_spec(norm_g.shape),
                  _const_spec(tri.shape), _const_spec(ecol.shape), _const_spec(e64.shape),
                  _const_spec(eye2.shape), _const_spec(headmask.shape)],
        out_specs=pl.BlockSpec((SSD_BLOCK, D_SSD), fwd),
        out_shape=jax.ShapeDtypeStruct((t, D_SSD), BF16),
        scratch_shapes=[state, pltpu.VMEM((SSD_STEP_CHUNKS, 3, CHUNK, LANES), BF16)],
        compiler_params=_params(("parallel", "arbitrary")),
        name="ssd_fwd",
    )(xbcs, dt, gate, yb, dt_bias, a_log, dskip, norm_g, tri, ecol, e64, eye2, headmask)


def _mix_ffn_kernel(nblk, xp_ref, xc_ref, xn_ref, ap_ref, ac_ref, an_ref, sp_ref, sc_ref, sn_ref,
                    wa_ref, ws_ref, gpost_ref, gpre_ref, wg_ref, wu_ref, wd_ref, cw_ref, cb_ref,
                    gffn_ref, o_ref, xext, aext, sext, hext, gbuf, act):
    i = pl.program_id(0) % nblk
    main = slice(HALO, HALO + TOK_BLOCK)
    for ext, (p_ref, c_ref, n_ref) in ((xext, (xp_ref, xc_ref, xn_ref)), (aext, (ap_ref, ac_ref, an_ref)),
                                       (sext, (sp_ref, sc_ref, sn_ref))):
        ext[0:HALO, :] = p_ref[...]
        ext[main, :] = c_ref[...]
        ext[HALO + TOK_BLOCK:, :] = n_ref[...]
    mix = (jnp.dot(aext[...], wa_ref[...], preferred_element_type=F32)
           + jnp.dot(sext[...], ws_ref[...], preferred_element_type=F32))
    x1 = xext[...] + _rms(mix, gpost_ref[...])
    xext[...] = x1
    hext[...] = _rms(x1, gpre_ref[...]).astype(BF16)
    keep_prev = jnp.where(i == 0, 0.0, 1.0)
    keep_next = jnp.where(i == nblk - 1, 0.0, 1.0)
    row = lax.broadcasted_iota(jnp.int32, (TOK_BLOCK, FF_COLS), 0)
    for c in range(D_FF // FF_COLS):
        cols = slice(c * FF_COLS, (c + 1) * FF_COLS)
        gbuf[...] = jnp.dot(hext[...], wg_ref[:, cols], preferred_element_type=F32)
        up = jnp.dot(hext[main, :], wu_ref[:, cols], preferred_element_type=F32)
        gp = gbuf[HALO - 1:HALO - 1 + TOK_BLOCK, :]
        gc = gbuf[main, :]
        gn = gbuf[HALO + 1:HALO + 1 + TOK_BLOCK, :]
        gp = jnp.where(row == 0, gp * keep_prev, gp)
        gn = jnp.where(row == TOK_BLOCK - 1, gn * keep_next, gn)
        gate = (gp * cw_ref[0:1, cols] + gc * cw_ref[1:2, cols] + gn * cw_ref[2:3, cols]
                + cb_ref[:, cols])
        gelu = 0.5 * gate * (1.0 + jnp.tanh(0.7978845608028654 * (gate + 0.044715 * gate * gate * gate)))
        act[:, cols] = (gelu * up).astype(BF16)
    f = jnp.dot(act[...], wd_ref[...], preferred_element_type=F32)
    o_ref[...] = xext[main, :] + _rms(f, gffn_ref[...])


def _mix_ffn(x2, att, ssd, p, seq):
    t = x2.shape[0]
    nblk = seq // TOK_BLOCK
    per = TOK_BLOCK // HALO
    last = t // HALO - 1
    row = lambda i: (i, 0)
    prev = lambda i: (jnp.maximum(i * per - 1, 0), 0)
    nxt = lambda i: (jnp.minimum((i + 1) * per, last), 0)

    def halo_specs(width):
        return [pl.BlockSpec((HALO, width), prev), pl.BlockSpec((TOK_BLOCK, width), row),
                pl.BlockSpec((HALO, width), nxt)]

    weights = [p[n] for n in ("wo_a", "wo_s", "g_mix_post", "g_ffn_pre", "wg", "wu", "wd",
                              "ffn_cw", "ffn_cb", "g_ffn_post")]
    ext_rows = TOK_BLOCK + 2 * HALO
    return pl.pallas_call(
        functools.partial(_mix_ffn_kernel, nblk),
        grid=(t // TOK_BLOCK,),
        in_specs=(halo_specs(D_MODEL) + halo_specs(D_ATTN) + halo_specs(D_SSD)
                  + [_resident_spec(w.shape) for w in weights]),
        out_specs=pl.BlockSpec((TOK_BLOCK, D_MODEL), row),
        out_shape=jax.ShapeDtypeStruct((t, D_MODEL), F32),
        scratch_shapes=[pltpu.VMEM((ext_rows, D_MODEL), F32), pltpu.VMEM((ext_rows, D_ATTN), BF16),
                        pltpu.VMEM((ext_rows, D_SSD), BF16), pltpu.VMEM((ext_rows, D_MODEL), BF16),
                        pltpu.VMEM((ext_rows, FF_COLS), F32), pltpu.VMEM((TOK_BLOCK, D_FF), BF16)],
        compiler_params=_params(("parallel",)),
        name="mix_ffn",
    )(x2, x2, x2, att, att, att, ssd, ssd, ssd, *weights)


def _dt_tile(v):
    return jnp.tile(v, (1,) * (v.ndim - 1) + (DT_REP,))


def _prepare(norm_mix_pre, norm_mix_post, w_in, attn_rpb, attn_out_norm, ssd_conv_w, ssd_conv_b,
             ssd_dt_bias, ssd_a_log, ssd_d, ssd_norm, w_out, norm_ffn_pre, norm_ffn_post,
             w_gate, w_up, ffn_conv_w, ffn_conv_b, w_down):
    row = lambda v: v.reshape(1, -1).astype(F32)
    o_z = 3 * D_ATTN
    o_x = o_z + D_SSD
    o_dt = o_x + D_XBC
    scale = ATTN_HEAD_DIM ** -0.5
    wqkv = jnp.concatenate([w_in[:, :D_ATTN] * scale, w_in[:, D_ATTN:o_z]], axis=1)
    return dict(
        g_mix_pre=row(norm_mix_pre), g_mix_post=row(norm_mix_post),
        wqkv=wqkv.astype(BF16), wz=w_in[:, o_z:o_x].astype(BF16), wxbc=w_in[:, o_x:o_dt].astype(BF16),
        wdt=_dt_tile(w_in[:, o_dt:]).astype(BF16),
        bias_tab=_attn_bias_table(attn_rpb), g_attn=row(attn_out_norm),
        conv_w=ssd_conv_w.astype(F32), conv_b=row(ssd_conv_b),
        dt_bias=_dt_tile(row(ssd_dt_bias)), a_log=_dt_tile(row(ssd_a_log)),
        dskip=row(jnp.repeat(ssd_d, SSD_HEAD_DIM)), g_ssd=row(ssd_norm),
        wo_a=w_out[:D_ATTN].astype(BF16), wo_s=w_out[D_ATTN:].astype(BF16),
        g_ffn_pre=row(norm_ffn_pre), g_ffn_post=row(norm_ffn_post),
        wg=w_gate.astype(BF16), wu=w_up.astype(BF16), wd=w_down.astype(BF16),
        ffn_cw=ffn_conv_w.astype(F32), ffn_cb=row(ffn_conv_b),
    )


def _layer(x, p, consts):
    batch, seq, _ = x.shape
    x2 = x.reshape(batch * seq, D_MODEL)
    qkv, gate, xbcs, dt = _inproj(x2, p, seq)
    att = _attention(qkv, p["bias_tab"], p["g_attn"], batch, seq)
    ssd = _ssd(xbcs, dt, gate, p["dt_bias"], p["a_log"], p["dskip"], p["g_ssd"], consts, batch, seq)
    y = _mix_ffn(x2, att, ssd, p, seq)
    return y.reshape(batch, seq, D_MODEL)


def kernel(x_prompt, x_sample, norm_mix_pre, norm_mix_post, w_in, attn_rpb, attn_out_norm, ssd_conv_w, ssd_conv_b, ssd_dt_bias, ssd_a_log, ssd_d, ssd_norm, w_out, norm_ffn_pre, norm_ffn_post, w_gate, w_up, ffn_conv_w, ffn_conv_b, w_down):
    assert norm_mix_pre.shape[0] == 1, "single-layer trunk"
    p = _prepare(norm_mix_pre[0], norm_mix_post[0], w_in[0], attn_rpb[0], attn_out_norm[0],
                 ssd_conv_w[0], ssd_conv_b[0], ssd_dt_bias[0], ssd_a_log[0], ssd_d[0], ssd_norm[0],
                 w_out[0], norm_ffn_pre[0], norm_ffn_post[0], w_gate[0], w_up[0],
                 ffn_conv_w[0], ffn_conv_b[0], w_down[0])
    consts = _ssd_constants()
    return (_layer(x_prompt, p, consts), _layer(x_sample, p, consts))
```

```python
import functools

import numpy as np
import jax
import jax.numpy as jnp
from jax import lax
from jax.experimental import pallas as pl
from jax.experimental.pallas import tpu as pltpu

F32 = jnp.float32
BF16 = jnp.bfloat16

D_MODEL = 1024
GRID_W = 64
N_ATTN_HEADS = 8
ATTN_HEAD_DIM = 64
D_ATTN = N_ATTN_HEADS * ATTN_HEAD_DIM
WIN_ROWS = 8
WIN_COLS = 16
D_SSD = 1024
SSD_HEAD_DIM = 64
N_SSD_HEADS = 16
SSD_GROUPS = 2
SSD_STATE = 128
SSD_CONV_W = 5
CHUNK = 128
D_BC = SSD_GROUPS * SSD_STATE
D_XBC = D_SSD + 2 * D_BC
D_FF = 2816
FFN_CONV_W = 3
EPS = 1e-6
LOG2E = 1.4426950408889634

LANES = 128
BF16_SUBLANES = 16
MXU_N = 256
VMEM_LIMIT = 56 * 1024 * 1024

TOK_BLOCK = 512
ATT_ROWS = 8
ATT_BLOCK = ATT_ROWS * GRID_W
ATT_WIN_ROWS = 3 * ATT_ROWS
HEADS_PER_TILE = LANES // ATTN_HEAD_DIM
N_HEAD_TILES = N_ATTN_HEADS // HEADS_PER_TILE
N_BIAS_TYPES = WIN_ROWS
MASK_VALUE = -1e30
HALO = BF16_SUBLANES
DT_REP = LANES // (2 * N_SSD_HEADS)
DIAG_HEADS = 4
FF_COLS = MXU_N
IN_CONV_COLS = MXU_N


def _params(sem):
    return pltpu.CompilerParams(dimension_semantics=sem, vmem_limit_bytes=VMEM_LIMIT)


def _const_spec(shape):
    nd = len(shape)
    return pl.BlockSpec(shape, lambda *_: (0,) * nd)


def _resident_spec(shape):
    nd = len(shape)
    return pl.BlockSpec(shape, lambda *_: (0,) * nd, pipeline_mode=pl.Buffered(1))


def _rms(x, gain):
    return x * lax.rsqrt(jnp.mean(x * x, axis=-1, keepdims=True) + EPS) * gain


def _split3(v):
    hi = v.astype(BF16)
    r1 = v - hi.astype(F32)
    mid = r1.astype(BF16)
    lo = (r1 - mid.astype(F32)).astype(BF16)
    return hi, mid, lo


def _silu(v):
    return v * (1.0 / (1.0 + jnp.exp(-v)))


def _inproj_kernel(nblk, xp_ref, xc_ref, xn_ref, g_ref, wqkv_ref, wz_ref, wxbc_ref, wdt_ref,
                   cw_ref, cbias_ref, qkv_ref, z_ref, xbc_ref, dt_ref, hext, cbuf0, cbuf1):
    i = pl.program_id(0) % nblk
    gain = g_ref[...]
    main = slice(HALO, HALO + TOK_BLOCK)
    hext[0:HALO, :] = (_rms(xp_ref[...], gain) * jnp.where(i == 0, 0.0, 1.0)).astype(BF16)
    hext[main, :] = _rms(xc_ref[...], gain).astype(BF16)
    hext[HALO + TOK_BLOCK:, :] = (_rms(xn_ref[...], gain) * jnp.where(i == nblk - 1, 0.0, 1.0)).astype(BF16)
    pad = SSD_CONV_W // 2

    def xbc_cols(c, buf):
        cols = slice(c * IN_CONV_COLS, (c + 1) * IN_CONV_COLS)
        buf[...] = jnp.dot(hext[...], wxbc_ref[:, cols], preferred_element_type=F32)

    def conv_cols(c, buf):
        cols = slice(c * IN_CONV_COLS, (c + 1) * IN_CONV_COLS)
        acc = cbias_ref[:, cols] + buf[HALO - pad:HALO - pad + TOK_BLOCK, :] * cw_ref[0:1, cols]
        for tap in range(1, SSD_CONV_W):
            acc = acc + buf[HALO - pad + tap:HALO - pad + tap + TOK_BLOCK, :] * cw_ref[tap:tap + 1, cols]
        xbc_ref[:, cols] = _silu(acc).astype(BF16)

    def qkv_dot():
        qkv_ref[...] = jnp.dot(hext[main, :], wqkv_ref[...], preferred_element_type=F32).astype(BF16)

    def z_dot():
        z_ref[...] = _silu(jnp.dot(hext[main, :], wz_ref[...], preferred_element_type=F32)).astype(BF16)

    def dt_dot():
        dt_ref[...] = jnp.dot(hext[main, :], wdt_ref[...], preferred_element_type=F32)

    bufs = (cbuf0, cbuf1)
    n_conv = D_XBC // IN_CONV_COLS
    others = {0: dt_dot, n_conv // 3: qkv_dot, n_conv - 1: z_dot}
    xbc_cols(0, bufs[0])
    for c in range(n_conv):
        if c + 1 < n_conv:
            xbc_cols(c + 1, bufs[(c + 1) % 2])
        if c in others:
            others[c]()
        conv_cols(c, bufs[c % 2])


def _inproj(x2, p, seq):
    t = x2.shape[0]
    nblk = seq // TOK_BLOCK
    per = TOK_BLOCK // HALO
    last = t // HALO - 1
    row = lambda i: (i, 0)
    weights = [p[n] for n in ("g_mix_pre", "wqkv", "wz", "wxbc", "wdt", "conv_w", "conv_b")]
    ext_rows = TOK_BLOCK + 2 * HALO
    return pl.pallas_call(
        functools.partial(_inproj_kernel, nblk),
        grid=(t // TOK_BLOCK,),
        in_specs=[pl.BlockSpec((HALO, D_MODEL), lambda i: (jnp.maximum(i * per - 1, 0), 0)),
                  pl.BlockSpec((TOK_BLOCK, D_MODEL), row),
                  pl.BlockSpec((HALO, D_MODEL), lambda i: (jnp.minimum((i + 1) * per, last), 0))]
                 + [_resident_spec(w.shape) for w in weights],
        out_specs=[pl.BlockSpec((TOK_BLOCK, 3 * D_ATTN), row),
                   pl.BlockSpec((TOK_BLOCK, D_SSD), row),
                   pl.BlockSpec((TOK_BLOCK, D_XBC), row),
                   pl.BlockSpec((TOK_BLOCK, LANES), row)],
        out_shape=[jax.ShapeDtypeStruct((t, 3 * D_ATTN), BF16),
                   jax.ShapeDtypeStruct((t, D_SSD), BF16),
                   jax.ShapeDtypeStruct((t, D_XBC), BF16),
                   jax.ShapeDtypeStruct((t, LANES), F32)],
        scratch_shapes=[pltpu.VMEM((ext_rows, D_MODEL), BF16),
                        pltpu.VMEM((ext_rows, IN_CONV_COLS), F32), pltpu.VMEM((ext_rows, IN_CONV_COLS), F32)],
        compiler_params=_params(("parallel",)),
        name="inproj",
    )(x2, x2, x2, *weights)


def _attn_window_base(j, rows):
    return jnp.clip((j - 1) * ATT_ROWS, 0, rows - ATT_WIN_ROWS)


def _attn_kernel(rows, q_ref, kwin, vwin, bias_ref, g_ref, o_ref):
    j = pl.program_id(1)
    base = _attn_window_base(j, rows)
    lane = lax.broadcasted_iota(jnp.int32, (GRID_W, LANES), 1)
    first_head = lane < ATTN_HEAD_DIM
    gain = g_ref[...]
    ones = jnp.ones((WIN_ROWS * GRID_W, LANES), BF16)
    n_keys = WIN_ROWS * GRID_W

    def window(i):
        r = j * ATT_ROWS + i
        rs = jnp.clip(r - WIN_ROWS // 2, 0, rows - WIN_ROWS)
        start = pl.multiple_of((rs - base) * GRID_W, GRID_W)
        return start, rs - r + (WIN_ROWS - 1)

    def scores(i, hp, start, btype):
        cols = slice(hp * LANES, (hp + 1) * LANES)
        q2 = q_ref[i * GRID_W:(i + 1) * GRID_W, cols]
        zero = jnp.zeros_like(q2)
        qbd = jnp.concatenate([jnp.where(first_head, q2, zero),
                               jnp.where(first_head, zero, q2)], axis=0)
        kp = kwin[pl.ds(start, n_keys), cols]
        s = lax.dot_general(qbd, kp, (((1,), (1,)), ((), ())), preferred_element_type=F32)
        bias = jnp.concatenate([bias_ref[hp, btype + 2 * t] for t in range(WIN_ROWS // 2)], axis=1)
        return s + bias

    def probs(s):
        return jnp.exp(s - jnp.max(s, axis=-1, keepdims=True)).astype(BF16)

    def weighted(hp, start, p):
        cols = slice(hp * LANES, (hp + 1) * LANES)
        v1 = jnp.concatenate([vwin[pl.ds(start, n_keys), cols], ones], axis=1)
        o = jnp.dot(p, v1, preferred_element_type=F32)
        o = o[:, :LANES] / o[:, LANES:]
        return jnp.where(first_head, o[:GRID_W], o[GRID_W:])

    items = [(i, hp) for i in range(ATT_ROWS) for hp in range(N_HEAD_TILES)]
    wins = [window(i) for i in range(ATT_ROWS)]
    s_val, p_val, outs = {}, {}, {}
    for n in range(len(items) + 2):
        if n < len(items):
            i, hp = items[n]
            s_val[n] = scores(i, hp, *wins[i])
        if 1 <= n <= len(items):
            p_val[n - 1] = probs(s_val.pop(n - 1))
        if n >= 2:
            i, hp = items[n - 2]
            outs[hp] = weighted(hp, wins[i][0], p_val.pop(n - 2))
            if hp == N_HEAD_TILES - 1:
                att = jnp.concatenate([outs[h] for h in range(N_HEAD_TILES)], axis=1)
                o_ref[i * GRID_W:(i + 1) * GRID_W, :] = _rms(att, gain).astype(BF16)


def _attention(qkv, bias_tab, gain, batch, seq):
    t = qkv.shape[0]
    rows = seq // GRID_W
    nblk = seq // ATT_BLOCK

    assert rows >= ATT_WIN_ROWS

    def kv_spec(col):
        return pl.BlockSpec(
            (pl.Element(ATT_WIN_ROWS * GRID_W), pl.Element(D_ATTN)),
            lambda b, j: (pl.multiple_of(b * seq + _attn_window_base(j, rows) * GRID_W, GRID_W),
                          col * D_ATTN))

    return pl.pallas_call(
        functools.partial(_attn_kernel, rows),
        grid=(batch, nblk),
        in_specs=[pl.BlockSpec((ATT_BLOCK, D_ATTN), lambda b, j: (b * nblk + j, 0)),
                  kv_spec(1), kv_spec(2),
                  _const_spec(bias_tab.shape), _const_spec(gain.shape)],
        out_specs=pl.BlockSpec((ATT_BLOCK, D_ATTN), lambda b, j: (b * nblk + j, 0)),
        out_shape=jax.ShapeDtypeStruct((t, D_ATTN), BF16),
        compiler_params=_params(("parallel", "arbitrary")),
        name="natten",
    )(qkv, qkv, qkv, bias_tab, gain)


def _attn_bias_table(rpb):
    pad = GRID_W - WIN_COLS
    padded = jnp.pad(rpb.astype(F32), ((0, 0), (0, 0), (pad, pad)))
    first = WIN_COLS - 1 + pad
    toeplitz = jnp.stack([padded[:, :, first - q:first - q + GRID_W] for q in range(GRID_W)], axis=2)
    qc = np.arange(GRID_W)[:, None]
    kc = np.arange(GRID_W)[None, :]
    cs = np.clip(qc - WIN_COLS // 2, 0, GRID_W - WIN_COLS)
    valid = (kc >= cs) & (kc < cs + WIN_COLS)
    toeplitz = jnp.where(jnp.asarray(valid), toeplitz, MASK_VALUE)
    pair = jnp.concatenate([toeplitz[:, :-1], toeplitz[:, 1:]], axis=-1)
    n_off = pair.shape[1]
    pair = pair.reshape(N_HEAD_TILES, HEADS_PER_TILE, n_off, GRID_W, LANES).transpose(0, 2, 1, 3, 4)
    return pair.reshape(N_HEAD_TILES, n_off, HEADS_PER_TILE * GRID_W, LANES)


SSD_STEP_CHUNKS = 8
SSD_BLOCK = SSD_STEP_CHUNKS * CHUNK
FWD_LOOKAHEAD = 4
BWD_LOOKAHEAD = 2


def _ssd_constants():
    lane = np.arange(LANES)
    m, d, k = lane // 32, (lane // 16) % 2, lane % 16
    tri_l = (np.arange(CHUNK)[:, None] >= np.arange(CHUNK)[None, :]).astype(np.float32)
    tri = np.concatenate([np.tile(tri_l, (1, 3)), np.tile(tri_l.T, (1, 3))], axis=0)
    ecol = np.zeros((LANES, 2 * N_SSD_HEADS * CHUNK), np.float32)
    e64 = np.zeros((LANES, 2 * D_SSD), np.float32)
    for ln in range(LANES):
        if m[ln] < 3:
            c0 = (d[ln] * N_SSD_HEADS + k[ln]) * CHUNK
            ecol[ln, c0:c0 + CHUNK] = 1.0
            c1 = d[ln] * D_SSD + k[ln] * SSD_HEAD_DIM
            e64[ln, c1:c1 + SSD_HEAD_DIM] = 1.0
    eye2 = np.concatenate([np.eye(CHUNK), np.eye(CHUNK)], axis=1).astype(np.float32)
    headmask = (np.arange(DIAG_HEADS)[:, None] == np.arange(DIAG_HEADS * SSD_HEAD_DIM)[None, :] // SSD_HEAD_DIM)
    return dict(tri=jnp.asarray(tri, BF16), ecol=jnp.asarray(ecol, BF16), e64=jnp.asarray(e64, BF16),
                eye2=jnp.asarray(eye2, BF16), headmask=jnp.asarray(headmask.astype(np.float32), BF16))


def _pieces_by_replica(v):
    hi, mid, lo = _split3(v)
    rep = lax.broadcasted_iota(jnp.int32, v.shape, 1) >> 5
    zero = jnp.zeros_like(hi)
    return jnp.where(rep == 0, hi, jnp.where(rep == 1, mid, jnp.where(rep == 2, lo, zero)))


def _chunk_rows(ci):
    return slice(ci * CHUNK, (ci + 1) * CHUNK)


def _ssd_dt_prep(ci, dt_ref, dtb_ref, alog_ref, tri_ref):
    x = dt_ref[_chunk_rows(ci), :] + dtb_ref[...]
    dt = jnp.maximum(x, 0.0) + jnp.log(1.0 + jnp.exp(-jnp.abs(x)))
    da = dt * (-LOG2E * jnp.exp(alog_ref[...]))
    hi, mid, lo = _split3(da)
    cs = jnp.dot(tri_ref[...], jnp.concatenate([hi, mid, lo], axis=0), preferred_element_type=F32)
    bwd_lane = ((lax.broadcasted_iota(jnp.int32, (CHUNK, LANES), 1) >> 4) & 1) == 1
    a = jnp.where(bwd_lane, cs[CHUNK:], cs[:CHUNK])
    return dt, a, bwd_lane


def _chunk_totals(a, bwd_lane):
    return jnp.where(bwd_lane[:1], a[:1], a[CHUNK - 1:])


def _state_step(state_ref, g, b_g, x_g, w_exp_g, decay_row):
    xw = x_g * w_exp_g.astype(BF16)
    bt = b_g.astype(F32).T.astype(BF16)
    state_ref[g] = state_ref[g] * decay_row + jnp.dot(bt, xw, preferred_element_type=F32)


def _ssd_bwd_kernel(xbc_ref, dt_ref, dtb_ref, alog_ref, tri_ref, e64_ref, yb_ref, state):
    @pl.when(pl.program_id(1) == 0)
    def _():
        state[...] = jnp.zeros_like(state)

    half = D_SSD // SSD_GROUPS

    def prep(ci):
        dt, a, bwd_lane = _ssd_dt_prep(ci, dt_ref, dtb_ref, alog_ref, tri_ref)
        ea_pieces = _pieces_by_replica(jnp.exp2(a))
        w_pieces = _pieces_by_replica(jnp.exp2(_chunk_totals(a, bwd_lane) - a) * dt)
        out = []
        for g in range(SSD_GROUPS):
            bs = slice(D_SSD + g * half, D_SSD + (g + 1) * half)
            out.append((jnp.dot(ea_pieces, e64_ref[:, bs], preferred_element_type=F32),
                        jnp.dot(w_pieces, e64_ref[:, bs], preferred_element_type=F32)))
        return out

    def state_part(ci, broadcasts):
        rows = _chunk_rows(ci)
        for g in range(SSD_GROUPS):
            hs = slice(g * half, (g + 1) * half)
            b_g = xbc_ref[rows, D_SSD + g * SSD_STATE:D_SSD + (g + 1) * SSD_STATE]
            c_g = xbc_ref[rows, D_SSD + D_BC + g * SSD_STATE:D_SSD + D_BC + (g + 1) * SSD_STATE]
            ea_exp, w_exp = broadcasts[g]
            y = jnp.dot(c_g, state[g].astype(BF16), preferred_element_type=F32)
            yb_ref[rows, hs] = (y * ea_exp).astype(BF16)
            _state_step(state, g, b_g, xbc_ref[rows, hs], w_exp, ea_exp[0:1, :])

    order = list(reversed(range(SSD_STEP_CHUNKS)))
    preps = {ci: prep(ci) for ci in order[:BWD_LOOKAHEAD]}
    for n, ci in enumerate(order):
        if n + BWD_LOOKAHEAD < len(order):
            nxt = order[n + BWD_LOOKAHEAD]
            preps[nxt] = prep(nxt)
        state_part(ci, preps.pop(ci))


def _ssd_fwd_kernel(xbc_ref, dt_ref, z_ref, yb_ref, dtb_ref, alog_ref, dskip_ref, ng_ref,
                    tri_ref, ecol_ref, e64_ref, eye2_ref, headmask_ref, o_ref, state, pieces):
    @pl.when(pl.program_id(1) == 0)
    def _():
        state[...] = jnp.zeros_like(state)

    dskip = dskip_ref[...].astype(BF16)
    lane = lax.broadcasted_iota(jnp.int32, (CHUNK, LANES), 1)
    li = lax.broadcasted_iota(jnp.int32, (CHUNK, CHUNK), 0)
    si = lax.broadcasted_iota(jnp.int32, (CHUNK, CHUNK), 1)
    below = si < li
    above = si > li
    half = D_SSD // SSD_GROUPS
    heads_per_group = N_SSD_HEADS // SSD_GROUPS
    wide = DIAG_HEADS * SSD_HEAD_DIM
    bwd0 = N_SSD_HEADS * CHUNK

    def prep(ci):
        dt, a, bwd_lane = _ssd_dt_prep(ci, dt_ref, dtb_ref, alog_ref, tri_ref)
        gq = a - LOG2E * jnp.log(dt)
        dt_sum = dt + pltpu.roll(dt, 16, axis=1)
        rows_t = jnp.where(lane < 32, gq, LOG2E * jnp.log(dt_sum)).T
        w = jnp.exp2(_chunk_totals(a, bwd_lane) - gq)
        rows = _chunk_rows(ci)
        cbs = []
        for g in range(SSD_GROUPS):
            b_g = xbc_ref[rows, D_SSD + g * SSD_STATE:D_SSD + (g + 1) * SSD_STATE]
            c_g = xbc_ref[rows, D_SSD + D_BC + g * SSD_STATE:D_SSD + D_BC + (g + 1) * SSD_STATE]
            cbs.append(lax.dot_general(c_g, b_g, (((1,), (1,)), ((), ())), preferred_element_type=F32))
        pieces[ci, 0] = _pieces_by_replica(a)
        pieces[ci, 1] = _pieces_by_replica(jnp.exp2(a))
        pieces[ci, 2] = _pieces_by_replica(w)
        return dict(ci=ci, rows_t=rows_t, cb=cbs)

    def acol_pair(p, k):
        pair = slice(k * CHUNK, (k + 2) * CHUNK)
        pair_b = slice(bwd0 + k * CHUNK, bwd0 + (k + 2) * CHUNK)
        a_pieces = pieces[p["ci"], 0]
        return (jnp.dot(a_pieces, ecol_ref[:, pair], preferred_element_type=F32),
                jnp.dot(a_pieces, ecol_ref[:, pair_b], preferred_element_type=F32))

    def m_pair(p, k, acol):
        rows_t, cb = p["rows_t"], p["cb"][k // heads_per_group]
        ms = []
        for hh in range(2):
            kk = k + hh
            cols = slice(hh * CHUNK, (hh + 1) * CHUNK)
            arg_f = acol[0][:, cols] - rows_t[kk:kk + 1, :]
            arg_b = acol[1][:, cols] - rows_t[16 + kk:17 + kk, :]
            arg = jnp.where(below, arg_f, jnp.where(above, arg_b, rows_t[32 + kk:33 + kk, :]))
            ms.append((jnp.exp2(arg) * cb).astype(BF16))
        return ms

    def diag_block(ci, k0, ms):
        rows = _chunk_rows(ci)
        qs = slice(k0 * SSD_HEAD_DIM, k0 * SSD_HEAD_DIM + wide)
        xq = xbc_ref[rows, qs]
        lhs = jnp.concatenate(ms + [eye2_ref[...]], axis=1)
        rhs = jnp.concatenate([xq * headmask_ref[hh:hh + 1, :] for hh in range(DIAG_HEADS)]
                              + [xq * dskip[:, qs], yb_ref[rows, qs]], axis=0)
        return jnp.dot(lhs, rhs, preferred_element_type=F32)

    def head_broadcasts(ci):
        return [(jnp.dot(pieces[ci, 1], e64_ref[:, g * half:(g + 1) * half], preferred_element_type=F32),
                 jnp.dot(pieces[ci, 2], e64_ref[:, g * half:(g + 1) * half], preferred_element_type=F32))
                for g in range(SSD_GROUPS)]

    def carried(ci, p, diag):
        rows = _chunk_rows(ci)
        per_group = heads_per_group // DIAG_HEADS
        for g in range(SSD_GROUPS):
            hs = slice(g * half, (g + 1) * half)
            b_g = xbc_ref[rows, D_SSD + g * SSD_STATE:D_SSD + (g + 1) * SSD_STATE]
            c_g = xbc_ref[rows, D_SSD + D_BC + g * SSD_STATE:D_SSD + D_BC + (g + 1) * SSD_STATE]
            y_diag = jnp.concatenate(diag[g * per_group:(g + 1) * per_group], axis=1)
            gate = z_ref[rows, hs].astype(F32)
            ea_exp, w_exp = p["exp"][g]
            y_off = jnp.dot(c_g, state[g].astype(BF16), preferred_element_type=F32) * ea_exp
            o_ref[rows, hs] = _rms((y_diag + y_off) * gate, ng_ref[:, hs]).astype(BF16)
            _state_step(state, g, b_g, xbc_ref[rows, hs], w_exp, ea_exp[CHUNK - 1:CHUNK, :])

    preps = [prep(ci) for ci in range(SSD_STEP_CHUNKS)]
    items = [(ci, k) for ci in range(SSD_STEP_CHUNKS) for k in range(0, N_SSD_HEADS, 2)]
    acols, ms, diag = {}, [], []
    for n in range(len(items) + FWD_LOOKAHEAD):
        if n < len(items):
            ci, k = items[n]
            acols[n] = acol_pair(preps[ci], k)
            if k == N_SSD_HEADS // 2:
                preps[ci]["exp"] = head_broadcasts(ci)
        m = n - FWD_LOOKAHEAD
        if m >= 0:
            ci, k = items[m]
            ms.extend(m_pair(preps[ci], k, acols.pop(m)))
            if len(ms) == DIAG_HEADS:
                diag.append(diag_block(ci, k + 2 - DIAG_HEADS, ms))
                ms = []
            if k + 2 == N_SSD_HEADS:
                carried(ci, preps[ci], diag)
                diag = []


def _ssd(xbcs, dt, gate, dt_bias, a_log, dskip, norm_g, consts, batch, seq):
    t = xbcs.shape[0]
    nblk = seq // SSD_BLOCK
    tri, ecol, e64, eye2, headmask = (consts[n] for n in ("tri", "ecol", "e64", "eye2", "headmask"))
    fwd = lambda b, c: (b * nblk + c, 0)
    bwd = lambda b, c: (b * nblk + (nblk - 1 - c), 0)
    state = pltpu.VMEM((SSD_GROUPS, SSD_STATE, D_SSD // SSD_GROUPS), F32)
    yb = pl.pallas_call(
        _ssd_bwd_kernel,
        grid=(batch, nblk),
        in_specs=[pl.BlockSpec((SSD_BLOCK, D_XBC), bwd), pl.BlockSpec((SSD_BLOCK, LANES), bwd),
                  _const_spec(dt_bias.shape), _const_spec(a_log.shape),
                  _const_spec(tri.shape), _const_spec(e64.shape)],
        out_specs=pl.BlockSpec((SSD_BLOCK, D_SSD), bwd),
        out_shape=jax.ShapeDtypeStruct((t, D_SSD), BF16),
        scratch_shapes=[state],
        compiler_params=_params(("parallel", "arbitrary")),
        name="ssd_bwd",
    )(xbcs, dt, dt_bias, a_log, tri, e64)
    return pl.pallas_call(
        _ssd_fwd_kernel,
        grid=(batch, nblk),
        in_specs=[pl.BlockSpec((SSD_BLOCK, D_XBC), fwd), pl.BlockSpec((SSD_BLOCK, LANES), fwd),
                  pl.BlockSpec((SSD_BLOCK, D_SSD), fwd), pl.BlockSpec((SSD_BLOCK, D_SSD), fwd),
                  _const_spec(dt_bias.shape), _const_spec(a_log.shape),
                  _const_spec(dskip.shape), _const_spec(norm_g.shape),
                  _const_spec(tri.shape), _const_spec(ecol.shape), _const_spec(e64.shape),
                  _const_spec(eye2.shape), _const_spec(headmask.shape)],
        out_specs=pl.BlockSpec((SSD_BLOCK, D_SSD), fwd),
        out_shape=jax.ShapeDtypeStruct((t, D_SSD), BF16),
        scratch_shapes=[state, pltpu.VMEM((SSD_STEP_CHUNKS, 3, CHUNK, LANES), BF16)],
        compiler_params=_params(("parallel", "arbitrary")),
        name="ssd_fwd",
    )(xbcs, dt, gate, yb, dt_bias, a_log, dskip, norm_g, tri, ecol, e64, eye2, headmask)


def _mix_ffn_kernel(nblk, xp_ref, xc_ref, xn_ref, ap_ref, ac_ref, an_ref, sp_ref, sc_ref, sn_ref,
                    wa_ref, ws_ref, gpost_ref, gpre_ref, wg_ref, wu_ref, wd_ref, cw_ref, cb_ref,
                    gffn_ref, o_ref, xext, aext, sext, hext, gbuf, act):
    i = pl.program_id(0) % nblk
    main = slice(HALO, HALO + TOK_BLOCK)
    for ext, (p_ref, c_ref, n_ref) in ((xext, (xp_ref, xc_ref, xn_ref)), (aext, (ap_ref, ac_ref, an_ref)),
                                       (sext, (sp_ref, sc_ref, sn_ref))):
        ext[0:HALO, :] = p_ref[...]
        ext[main, :] = c_ref[...]
        ext[HALO + TOK_BLOCK:, :] = n_ref[...]
    ext_half = (TOK_BLOCK + 2 * HALO) // 2
    for part in range(2):
        rows = slice(part * ext_half, (part + 1) * ext_half)
        mix = (jnp.dot(aext[rows, :], wa_ref[...], preferred_element_type=F32)
               + jnp.dot(sext[rows, :], ws_ref[...], preferred_element_type=F32))
        x1 = xext[rows, :] + _rms(mix, gpost_ref[...])
        xext[rows, :] = x1
        hext[rows, :] = _rms(x1, gpre_ref[...]).astype(BF16)
    keep_prev = jnp.where(i == 0, 0.0, 1.0)
    keep_next = jnp.where(i == nblk - 1, 0.0, 1.0)
    row = lax.broadcasted_iota(jnp.int32, (TOK_BLOCK, FF_COLS), 0)
    for c in range(D_FF // FF_COLS):
        cols = slice(c * FF_COLS, (c + 1) * FF_COLS)
        gbuf[...] = jnp.dot(hext[...], wg_ref[:, cols], preferred_element_type=F32)
        up = jnp.dot(hext[main, :], wu_ref[:, cols], preferred_element_type=F32)
        gp = gbuf[HALO - 1:HALO - 1 + TOK_BLOCK, :]
        gc = gbuf[main, :]
        gn = gbuf[HALO + 1:HALO + 1 + TOK_BLOCK, :]
        gp = jnp.where(row == 0, gp * keep_prev, gp)
        gn = jnp.where(row == TOK_BLOCK - 1, gn * keep_next, gn)
        gate = (gp * cw_ref[0:1, cols] + gc * cw_ref[1:2, cols] + gn * cw_ref[2:3, cols]
                + cb_ref[:, cols])
        gelu = 0.5 * gate * (1.0 + jnp.tanh(0.7978845608028654 * (gate + 0.044715 * gate * gate * gate)))
        act[:, cols] = (gelu * up).astype(BF16)
    for part in range(2):
        rows = slice(part * (TOK_BLOCK // 2), (part + 1) * (TOK_BLOCK // 2))
        f = jnp.dot(act[rows, :], wd_ref[...], preferred_element_type=F32)
        o_ref[rows, :] = xext[HALO + rows.start:HALO + rows.stop, :] + _rms(f, gffn_ref[...])


def _mix_ffn(x2, att, ssd, p, seq):
    t = x2.shape[0]
    nblk = seq // TOK_BLOCK
    per = TOK_BLOCK // HALO
    last = t // HALO - 1
    row = lambda i: (i, 0)
    prev = lambda i: (jnp.maximum(i * per - 1, 0), 0)
    nxt = lambda i: (jnp.minimum((i + 1) * per, last), 0)

    def halo_specs(width):
        return [pl.BlockSpec((HALO, width), prev), pl.BlockSpec((TOK_BLOCK, width), row),
                pl.BlockSpec((HALO, width), nxt)]

    weights = [p[n] for n in ("wo_a", "wo_s", "g_mix_post", "g_ffn_pre", "wg", "wu", "wd",
                              "ffn_cw", "ffn_cb", "g_ffn_post")]
    ext_rows = TOK_BLOCK + 2 * HALO
    return pl.pallas_call(
        functools.partial(_mix_ffn_kernel, nblk),
        grid=(t // TOK_BLOCK,),
        in_specs=(halo_specs(D_MODEL) + halo_specs(D_ATTN) + halo_specs(D_SSD)
                  + [_resident_spec(w.shape) for w in weights]),
        out_specs=pl.BlockSpec((TOK_BLOCK, D_MODEL), row),
        out_shape=jax.ShapeDtypeStruct((t, D_MODEL), F32),
        scratch_shapes=[pltpu.VMEM((ext_rows, D_MODEL), F32), pltpu.VMEM((ext_rows, D_ATTN), BF16),
                        pltpu.VMEM((ext_rows, D_SSD), BF16), pltpu.VMEM((ext_rows, D_MODEL), BF16),
                        pltpu.VMEM((ext_rows, FF_COLS), F32), pltpu.VMEM((TOK_BLOCK, D_FF), BF16)],
        compiler_params=_params(("parallel",)),
        name="mix_ffn",
    )(x2, x2, x2, att, att, att, ssd, ssd, ssd, *weights)


def _dt_tile(v):
    return jnp.tile(v, (1,) * (v.ndim - 1) + (DT_REP,))


def _prepare(norm_mix_pre, norm_mix_post, w_in, attn_rpb, attn_out_norm, ssd_conv_w, ssd_conv_b,
             ssd_dt_bias, ssd_a_log, ssd_d, ssd_norm, w_out, norm_ffn_pre, norm_ffn_post,
             w_gate, w_up, ffn_conv_w, ffn_conv_b, w_down):
    row = lambda v: v.reshape(1, -1).astype(F32)
    o_z = 3 * D_ATTN
    o_x = o_z + D_SSD
    o_dt = o_x + D_XBC
    scale = ATTN_HEAD_DIM ** -0.5
    wqkv = jnp.concatenate([w_in[:, :D_ATTN] * scale, w_in[:, D_ATTN:o_z]], axis=1)
    return dict(
        g_mix_pre=row(norm_mix_pre), g_mix_post=row(norm_mix_post),
        wqkv=wqkv.astype(BF16), wz=w_in[:, o_z:o_x].astype(BF16), wxbc=w_in[:, o_x:o_dt].astype(BF16),
        wdt=_dt_tile(w_in[:, o_dt:]).astype(BF16),
        bias_tab=_attn_bias_table(attn_rpb), g_attn=row(attn_out_norm),
        conv_w=ssd_conv_w.astype(F32), conv_b=row(ssd_conv_b),
        dt_bias=_dt_tile(row(ssd_dt_bias)), a_log=_dt_tile(row(ssd_a_log)),
        dskip=row(jnp.repeat(ssd_d, SSD_HEAD_DIM)), g_ssd=row(ssd_norm),
        wo_a=w_out[:D_ATTN].astype(BF16), wo_s=w_out[D_ATTN:].astype(BF16),
        g_ffn_pre=row(norm_ffn_pre), g_ffn_post=row(norm_ffn_post),
        wg=w_gate.astype(BF16), wu=w_up.astype(BF16), wd=w_down.astype(BF16),
        ffn_cw=ffn_conv_w.astype(F32), ffn_cb=row(ffn_conv_b),
    )


def _layer(x, p, consts):
    batch, seq, _ = x.shape
    x2 = x.reshape(batch * seq, D_MODEL)
    qkv, gate, xbcs, dt = _inproj(x2, p, seq)
    att = _attention(qkv, p["bias_tab"], p["g_attn"], batch, seq)
    ssd = _ssd(xbcs, dt, gate, p["dt_bias"], p["a_log"], p["dskip"], p["g_ssd"], consts, batch, seq)
    y = _mix_ffn(x2, att, ssd, p, seq)
    return y.reshape(batch, seq, D_MODEL)


def kernel(x_prompt, x_sample, norm_mix_pre, norm_mix_post, w_in, attn_rpb, attn_out_norm, ssd_conv_w, ssd_conv_b, ssd_dt_bias, ssd_a_log, ssd_d, ssd_norm, w_out, norm_ffn_pre, norm_ffn_post, w_gate, w_up, ffn_conv_w, ffn_conv_b, w_down):
    assert norm_mix_pre.shape[0] == 1, "single-layer trunk"
    p = _prepare(norm_mix_pre[0], norm_mix_post[0], w_in[0], attn_rpb[0], attn_out_norm[0],
                 ssd_conv_w[0], ssd_conv_b[0], ssd_dt_bias[0], ssd_a_log[0], ssd_d[0], ssd_norm[0],
                 w_out[0], norm_ffn_pre[0], norm_ffn_post[0], w_gate[0], w_up[0],
                 ffn_conv_w[0], ffn_conv_b[0], w_down[0])
    consts = _ssd_constants()
    return (_layer(x_prompt, p, consts), _layer(x_sample, p, consts))
```

```python
import functools

import numpy as np
import jax
import jax.numpy as jnp
from jax import lax
from jax.experimental import pallas as pl
from jax.experimental.pallas import tpu as pltpu

F32 = jnp.float32
BF16 = jnp.bfloat16

D_MODEL = 1024
GRID_W = 64
N_ATTN_HEADS = 8
ATTN_HEAD_DIM = 64
D_ATTN = N_ATTN_HEADS * ATTN_HEAD_DIM
WIN_ROWS = 8
WIN_COLS = 16
D_SSD = 1024
SSD_HEAD_DIM = 64
N_SSD_HEADS = 16
SSD_GROUPS = 2
SSD_STATE = 128
SSD_CONV_W = 5
CHUNK = 128
D_BC = SSD_GROUPS * SSD_STATE
D_XBC = D_SSD + 2 * D_BC
D_FF = 2816
FFN_CONV_W = 3
EPS = 1e-6
LOG2E = 1.4426950408889634

LANES = 128
BF16_SUBLANES = 16
MXU_N = 256
VMEM_LIMIT = 56 * 1024 * 1024

TOK_BLOCK = 512
IN_BLOCK = 1024
ATT_ROWS = 8
ATT_BLOCK = ATT_ROWS * GRID_W
ATT_WIN_ROWS = 3 * ATT_ROWS
HEADS_PER_TILE = LANES // ATTN_HEAD_DIM
N_HEAD_TILES = N_ATTN_HEADS // HEADS_PER_TILE
N_BIAS_TYPES = WIN_ROWS
MASK_VALUE = -1e30
HALO = BF16_SUBLANES
DT_REP = LANES // (2 * N_SSD_HEADS)
DIAG_HEADS = 4
FF_COLS = MXU_N
IN_CONV_COLS = MXU_N


def _params(sem):
    return pltpu.CompilerParams(dimension_semantics=sem, vmem_limit_bytes=VMEM_LIMIT)


def _const_spec(shape):
    nd = len(shape)
    return pl.BlockSpec(shape, lambda *_: (0,) * nd)


def _resident_spec(shape):
    nd = len(shape)
    return pl.BlockSpec(shape, lambda *_: (0,) * nd, pipeline_mode=pl.Buffered(1))


def _rms(x, gain):
    return x * lax.rsqrt(jnp.mean(x * x, axis=-1, keepdims=True) + EPS) * gain


def _split3(v):
    hi = v.astype(BF16)
    r1 = v - hi.astype(F32)
    mid = r1.astype(BF16)
    lo = (r1 - mid.astype(F32)).astype(BF16)
    return hi, mid, lo


def _silu(v):
    return v * (1.0 / (1.0 + jnp.exp(-v)))


def _inproj_kernel(nblk, xp_ref, xc_ref, xn_ref, g_ref, wqkv_ref, wz_ref, wxbc_ref, wdt_ref,
                   cw_ref, cbias_ref, qkv_ref, z_ref, xbc_ref, dt_ref, hext, cbuf0, cbuf1):
    TOK_BLOCK = IN_BLOCK
    i = pl.program_id(0) % nblk
    gain = g_ref[...]
    main = slice(HALO, HALO + TOK_BLOCK)
    hext[0:HALO, :] = (_rms(xp_ref[...], gain) * jnp.where(i == 0, 0.0, 1.0)).astype(BF16)
    hext[main, :] = _rms(xc_ref[...], gain).astype(BF16)
    hext[HALO + TOK_BLOCK:, :] = (_rms(xn_ref[...], gain) * jnp.where(i == nblk - 1, 0.0, 1.0)).astype(BF16)
    pad = SSD_CONV_W // 2

    def xbc_cols(c, buf):
        cols = slice(c * IN_CONV_COLS, (c + 1) * IN_CONV_COLS)
        buf[...] = jnp.dot(hext[...], wxbc_ref[:, cols], preferred_element_type=F32)

    def conv_cols(c, buf):
        cols = slice(c * IN_CONV_COLS, (c + 1) * IN_CONV_COLS)
        acc = cbias_ref[:, cols] + buf[HALO - pad:HALO - pad + TOK_BLOCK, :] * cw_ref[0:1, cols]
        for tap in range(1, SSD_CONV_W):
            acc = acc + buf[HALO - pad + tap:HALO - pad + tap + TOK_BLOCK, :] * cw_ref[tap:tap + 1, cols]
        xbc_ref[:, cols] = _silu(acc).astype(BF16)

    def qkv_dot():
        qkv_ref[...] = jnp.dot(hext[main, :], wqkv_ref[...], preferred_element_type=F32).astype(BF16)

    def z_dot():
        z_ref[...] = _silu(jnp.dot(hext[main, :], wz_ref[...], preferred_element_type=F32)).astype(BF16)

    def dt_dot():
        dt_ref[...] = jnp.dot(hext[main, :], wdt_ref[...], preferred_element_type=F32)

    bufs = (cbuf0, cbuf1)
    n_conv = D_XBC // IN_CONV_COLS
    others = {0: dt_dot, n_conv // 3: qkv_dot, n_conv - 1: z_dot}
    xbc_cols(0, bufs[0])
    for c in range(n_conv):
        if c + 1 < n_conv:
            xbc_cols(c + 1, bufs[(c + 1) % 2])
        if c in others:
            others[c]()
        conv_cols(c, bufs[c % 2])


def _inproj(x2, p, seq):
    TOK_BLOCK = IN_BLOCK
    t = x2.shape[0]
    nblk = seq // TOK_BLOCK
    per = TOK_BLOCK // HALO
    last = t // HALO - 1
    row = lambda i: (i, 0)
    weights = [p[n] for n in ("g_mix_pre", "wqkv", "wz", "wxbc", "wdt", "conv_w", "conv_b")]
    ext_rows = TOK_BLOCK + 2 * HALO
    return pl.pallas_call(
        functools.partial(_inproj_kernel, nblk),
        grid=(t // TOK_BLOCK,),
        in_specs=[pl.BlockSpec((HALO, D_MODEL), lambda i: (jnp.maximum(i * per - 1, 0), 0)),
                  pl.BlockSpec((TOK_BLOCK, D_MODEL), row),
                  pl.BlockSpec((HALO, D_MODEL), lambda i: (jnp.minimum((i + 1) * per, last), 0))]
                 + [_resident_spec(w.shape) for w in weights],
        out_specs=[pl.BlockSpec((TOK_BLOCK, 3 * D_ATTN), row),
                   pl.BlockSpec((TOK_BLOCK, D_SSD), row),
                   pl.BlockSpec((TOK_BLOCK, D_XBC), row),
                   pl.BlockSpec((TOK_BLOCK, LANES), row)],
        out_shape=[jax.ShapeDtypeStruct((t, 3 * D_ATTN), BF16),
                   jax.ShapeDtypeStruct((t, D_SSD), BF16),
                   jax.ShapeDtypeStruct((t, D_XBC), BF16),
                   jax.ShapeDtypeStruct((t, LANES), F32)],
        scratch_shapes=[pltpu.VMEM((ext_rows, D_MODEL), BF16),
                        pltpu.VMEM((ext_rows, IN_CONV_COLS), F32), pltpu.VMEM((ext_rows, IN_CONV_COLS), F32)],
        compiler_params=_params(("parallel",)),
        name="inproj",
    )(x2, x2, x2, *weights)


def _attn_window_base(j, rows):
    return jnp.clip((j - 1) * ATT_ROWS, 0, rows - ATT_WIN_ROWS)


def _attn_kernel(rows, q_ref, kwin, vwin, bias_ref, g_ref, o_ref):
    j = pl.program_id(1)
    base = _attn_window_base(j, rows)
    lane = lax.broadcasted_iota(jnp.int32, (GRID_W, LANES), 1)
    first_head = lane < ATTN_HEAD_DIM
    gain = g_ref[...]
    ones = jnp.ones((WIN_ROWS * GRID_W, LANES), BF16)
    n_keys = WIN_ROWS * GRID_W

    def window(i):
        r = j * ATT_ROWS + i
        rs = jnp.clip(r - WIN_ROWS // 2, 0, rows - WIN_ROWS)
        start = pl.multiple_of((rs - base) * GRID_W, GRID_W)
        return start, rs - r + (WIN_ROWS - 1)

    def scores(i, hp, start, btype):
        cols = slice(hp * LANES, (hp + 1) * LANES)
        q2 = q_ref[i * GRID_W:(i + 1) * GRID_W, cols]
        zero = jnp.zeros_like(q2)
        qbd = jnp.concatenate([jnp.where(first_head, q2, zero),
                               jnp.where(first_head, zero, q2)], axis=0)
        kp = kwin[pl.ds(start, n_keys), cols]
        s = lax.dot_general(qbd, kp, (((1,), (1,)), ((), ())), preferred_element_type=F32)
        bias = jnp.concatenate([bias_ref[hp, btype + 2 * t] for t in range(WIN_ROWS // 2)], axis=1)
        return s + bias

    def probs(s):
        return jnp.exp(s - jnp.max(s, axis=-1, keepdims=True)).astype(BF16)

    def weighted(hp, start, p):
        cols = slice(hp * LANES, (hp + 1) * LANES)
        v1 = jnp.concatenate([vwin[pl.ds(start, n_keys), cols], ones], axis=1)
        o = jnp.dot(p, v1, preferred_element_type=F32)
        o = o[:, :LANES] / o[:, LANES:]
        return jnp.where(first_head, o[:GRID_W], o[GRID_W:])

    items = [(i, hp) for i in range(ATT_ROWS) for hp in range(N_HEAD_TILES)]
    wins = [window(i) for i in range(ATT_ROWS)]
    s_val, p_val, outs = {}, {}, {}
    for n in range(len(items) + 2):
        if n < len(items):
            i, hp = items[n]
            s_val[n] = scores(i, hp, *wins[i])
        if 1 <= n <= len(items):
            p_val[n - 1] = probs(s_val.pop(n - 1))
        if n >= 2:
            i, hp = items[n - 2]
            outs[hp] = weighted(hp, wins[i][0], p_val.pop(n - 2))
            if hp == N_HEAD_TILES - 1:
                att = jnp.concatenate([outs[h] for h in range(N_HEAD_TILES)], axis=1)
                o_ref[i * GRID_W:(i + 1) * GRID_W, :] = _rms(att, gain).astype(BF16)


def _attention(qkv, bias_tab, gain, batch, seq):
    t = qkv.shape[0]
    rows = seq // GRID_W
    nblk = seq // ATT_BLOCK

    assert rows >= ATT_WIN_ROWS

    def kv_spec(col):
        return pl.BlockSpec(
            (pl.Element(ATT_WIN_ROWS * GRID_W), pl.Element(D_ATTN)),
            lambda b, j: (pl.multiple_of(b * seq + _attn_window_base(j, rows) * GRID_W, GRID_W),
                          col * D_ATTN))

    return pl.pallas_call(
        functools.partial(_attn_kernel, rows),
        grid=(batch, nblk),
        in_specs=[pl.BlockSpec((ATT_BLOCK, D_ATTN), lambda b, j: (b * nblk + j, 0)),
                  kv_spec(1), kv_spec(2),
                  _const_spec(bias_tab.shape), _const_spec(gain.shape)],
        out_specs=pl.BlockSpec((ATT_BLOCK, D_ATTN), lambda b, j: (b * nblk + j, 0)),
        out_shape=jax.ShapeDtypeStruct((t, D_ATTN), BF16),
        compiler_params=_params(("parallel", "arbitrary")),
        name="natten",
    )(qkv, qkv, qkv, bias_tab, gain)


def _attn_bias_table(rpb):
    pad = GRID_W - WIN_COLS
    padded = jnp.pad(rpb.astype(F32), ((0, 0), (0, 0), (pad, pad)))
    first = WIN_COLS - 1 + pad
    toeplitz = jnp.stack([padded[:, :, first - q:first - q + GRID_W] for q in range(GRID_W)], axis=2)
    qc = np.arange(GRID_W)[:, None]
    kc = np.arange(GRID_W)[None, :]
    cs = np.clip(qc - WIN_COLS // 2, 0, GRID_W - WIN_COLS)
    valid = (kc >= cs) & (kc < cs + WIN_COLS)
    toeplitz = jnp.where(jnp.asarray(valid), toeplitz, MASK_VALUE)
    pair = jnp.concatenate([toeplitz[:, :-1], toeplitz[:, 1:]], axis=-1)
    n_off = pair.shape[1]
    pair = pair.reshape(N_HEAD_TILES, HEADS_PER_TILE, n_off, GRID_W, LANES).transpose(0, 2, 1, 3, 4)
    return pair.reshape(N_HEAD_TILES, n_off, HEADS_PER_TILE * GRID_W, LANES)


SSD_STEP_CHUNKS = 8
SSD_BLOCK = SSD_STEP_CHUNKS * CHUNK
FWD_LOOKAHEAD = 4
BWD_LOOKAHEAD = 1


def _ssd_constants():
    lane = np.arange(LANES)
    m, d, k = lane // 32, (lane // 16) % 2, lane % 16
    tri_l = (np.arange(CHUNK)[:, None] >= np.arange(CHUNK)[None, :]).astype(np.float32)
    tri = np.concatenate([np.tile(tri_l, (1, 3)), np.tile(tri_l.T, (1, 3))], axis=0)
    ecol = np.zeros((LANES, 2 * N_SSD_HEADS * CHUNK), np.float32)
    e64 = np.zeros((LANES, 2 * D_SSD), np.float32)
    for ln in range(LANES):
        if m[ln] < 3:
            c0 = (d[ln] * N_SSD_HEADS + k[ln]) * CHUNK
            ecol[ln, c0:c0 + CHUNK] = 1.0
            c1 = d[ln] * D_SSD + k[ln] * SSD_HEAD_DIM
            e64[ln, c1:c1 + SSD_HEAD_DIM] = 1.0
    eye2 = np.concatenate([np.eye(CHUNK), np.eye(CHUNK)], axis=1).astype(np.float32)
    headmask = (np.arange(DIAG_HEADS)[:, None] == np.arange(DIAG_HEADS * SSD_HEAD_DIM)[None, :] // SSD_HEAD_DIM)
    return dict(tri=jnp.asarray(tri, BF16), ecol=jnp.asarray(ecol, BF16), e64=jnp.asarray(e64, BF16),
                eye2=jnp.asarray(eye2, BF16), headmask=jnp.asarray(headmask.astype(np.float32), BF16))


def _pieces_by_replica(v):
    hi, mid, lo = _split3(v)
    rep = lax.broadcasted_iota(jnp.int32, v.shape, 1) >> 5
    zero = jnp.zeros_like(hi)
    return jnp.where(rep == 0, hi, jnp.where(rep == 1, mid, jnp.where(rep == 2, lo, zero)))


def _chunk_rows(ci):
    return slice(ci * CHUNK, (ci + 1) * CHUNK)


def _ssd_dt_prep(ci, dt_ref, dtb_ref, alog_ref, tri_ref):
    x = dt_ref[_chunk_rows(ci), :] + dtb_ref[...]
    dt = jnp.maximum(x, 0.0) + jnp.log(1.0 + jnp.exp(-jnp.abs(x)))
    da = dt * (-LOG2E * jnp.exp(alog_ref[...]))
    hi, mid, lo = _split3(da)
    cs = jnp.dot(tri_ref[...], jnp.concatenate([hi, mid, lo], axis=0), preferred_element_type=F32)
    bwd_lane = ((lax.broadcasted_iota(jnp.int32, (CHUNK, LANES), 1) >> 4) & 1) == 1
    a = jnp.where(bwd_lane, cs[CHUNK:], cs[:CHUNK])
    return dt, a, bwd_lane


def _chunk_totals(a, bwd_lane):
    return jnp.where(bwd_lane[:1], a[:1], a[CHUNK - 1:])


def _state_step(state_ref, g, b_g, x_g, w_exp_g, decay_row):
    xw = x_g * w_exp_g.astype(BF16)
    bt = b_g.astype(F32).T.astype(BF16)
    state_ref[g] = state_ref[g] * decay_row + jnp.dot(bt, xw, preferred_element_type=F32)


def _ssd_bwd_kernel(xbc_ref, dt_ref, dtb_ref, alog_ref, tri_ref, e64_ref, yb_ref, state):
    @pl.when(pl.program_id(1) == 0)
    def _():
        state[...] = jnp.zeros_like(state)

    half = D_SSD // SSD_GROUPS

    def prep(ci):
        dt, a, bwd_lane = _ssd_dt_prep(ci, dt_ref, dtb_ref, alog_ref, tri_ref)
        ea_pieces = _pieces_by_replica(jnp.exp2(a))
        w_pieces = _pieces_by_replica(jnp.exp2(_chunk_totals(a, bwd_lane) - a) * dt)
        out = []
        for g in range(SSD_GROUPS):
            bs = slice(D_SSD + g * half, D_SSD + (g + 1) * half)
            out.append((jnp.dot(ea_pieces, e64_ref[:, bs], preferred_element_type=F32),
                        jnp.dot(w_pieces, e64_ref[:, bs], preferred_element_type=F32)))
        return out

    def state_part(ci, broadcasts):
        rows = _chunk_rows(ci)
        for g in range(SSD_GROUPS):
            hs = slice(g * half, (g + 1) * half)
            b_g = xbc_ref[rows, D_SSD + g * SSD_STATE:D_SSD + (g + 1) * SSD_STATE]
            c_g = xbc_ref[rows, D_SSD + D_BC + g * SSD_STATE:D_SSD + D_BC + (g + 1) * SSD_STATE]
            ea_exp, w_exp = broadcasts[g]
            y = jnp.dot(c_g, state[g].astype(BF16), preferred_element_type=F32)
            yb_ref[rows, hs] = (y * ea_exp).astype(BF16)
            _state_step(state, g, b_g, xbc_ref[rows, hs], w_exp, ea_exp[0:1, :])

    order = list(reversed(range(SSD_STEP_CHUNKS)))
    preps = {ci: prep(ci) for ci in order[:BWD_LOOKAHEAD]}
    for n, ci in enumerate(order):
        if n + BWD_LOOKAHEAD < len(order):
            nxt = order[n + BWD_LOOKAHEAD]
            preps[nxt] = prep(nxt)
        state_part(ci, preps.pop(ci))


def _ssd_fwd_kernel(xbc_ref, dt_ref, z_ref, yb_ref, dtb_ref, alog_ref, dskip_ref, ng_ref,
                    tri_ref, ecol_ref, e64_ref, eye2_ref, headmask_ref, o_ref, state, pieces):
    @pl.when(pl.program_id(1) == 0)
    def _():
        state[...] = jnp.zeros_like(state)

    dskip = dskip_ref[...].astype(BF16)
    lane = lax.broadcasted_iota(jnp.int32, (CHUNK, LANES), 1)
    li = lax.broadcasted_iota(jnp.int32, (CHUNK, CHUNK), 0)
    si = lax.broadcasted_iota(jnp.int32, (CHUNK, CHUNK), 1)
    below = si < li
    above = si > li
    half = D_SSD // SSD_GROUPS
    heads_per_group = N_SSD_HEADS // SSD_GROUPS
    wide = DIAG_HEADS * SSD_HEAD_DIM
    bwd0 = N_SSD_HEADS * CHUNK

    def prep(ci):
        dt, a, bwd_lane = _ssd_dt_prep(ci, dt_ref, dtb_ref, alog_ref, tri_ref)
        gq = a - LOG2E * jnp.log(dt)
        dt_sum = dt + pltpu.roll(dt, 16, axis=1)
        rows_t = jnp.where(lane < 32, gq, LOG2E * jnp.log(dt_sum)).T
        w = jnp.exp2(_chunk_totals(a, bwd_lane) - gq)
        rows = _chunk_rows(ci)
        cbs = []
        for g in range(SSD_GROUPS):
            b_g = xbc_ref[rows, D_SSD + g * SSD_STATE:D_SSD + (g + 1) * SSD_STATE]
            c_g = xbc_ref[rows, D_SSD + D_BC + g * SSD_STATE:D_SSD + D_BC + (g + 1) * SSD_STATE]
            cbs.append(lax.dot_general(c_g, b_g, (((1,), (1,)), ((), ())), preferred_element_type=F32))
        pieces[ci, 0] = _pieces_by_replica(a)
        pieces[ci, 1] = _pieces_by_replica(jnp.exp2(a))
        pieces[ci, 2] = _pieces_by_replica(w)
        return dict(ci=ci, rows_t=rows_t, cb=cbs)

    def acol_pair(p, k):
        pair = slice(k * CHUNK, (k + 2) * CHUNK)
        pair_b = slice(bwd0 + k * CHUNK, bwd0 + (k + 2) * CHUNK)
        a_pieces = pieces[p["ci"], 0]
        return (jnp.dot(a_pieces, ecol_ref[:, pair], preferred_element_type=F32),
                jnp.dot(a_pieces, ecol_ref[:, pair_b], preferred_element_type=F32))

    def m_pair(p, k, acol):
        rows_t, cb = p["rows_t"], p["cb"][k // heads_per_group]
        ms = []
        for hh in range(2):
            kk = k + hh
            cols = slice(hh * CHUNK, (hh + 1) * CHUNK)
            arg_f = acol[0][:, cols] - rows_t[kk:kk + 1, :]
            arg_b = acol[1][:, cols] - rows_t[16 + kk:17 + kk, :]
            arg = jnp.where(below, arg_f, jnp.where(above, arg_b, rows_t[32 + kk:33 + kk, :]))
            ms.append((jnp.exp2(arg) * cb).astype(BF16))
        return ms

    def diag_block(ci, k0, ms):
        rows = _chunk_rows(ci)
        qs = slice(k0 * SSD_HEAD_DIM, k0 * SSD_HEAD_DIM + wide)
        xq = xbc_ref[rows, qs]
        lhs = jnp.concatenate(ms + [eye2_ref[...]], axis=1)
        rhs = jnp.concatenate([xq * headmask_ref[hh:hh + 1, :] for hh in range(DIAG_HEADS)]
                              + [xq * dskip[:, qs], yb_ref[rows, qs]], axis=0)
        return jnp.dot(lhs, rhs, preferred_element_type=F32)

    def head_broadcasts(ci):
        return [(jnp.dot(pieces[ci, 1], e64_ref[:, g * half:(g + 1) * half], preferred_element_type=F32),
                 jnp.dot(pieces[ci, 2], e64_ref[:, g * half:(g + 1) * half], preferred_element_type=F32))
                for g in range(SSD_GROUPS)]

    def carried(ci, p, diag):
        rows = _chunk_rows(ci)
        per_group = heads_per_group // DIAG_HEADS
        for g in range(SSD_GROUPS):
            hs = slice(g * half, (g + 1) * half)
            b_g = xbc_ref[rows, D_SSD + g * SSD_STATE:D_SSD + (g + 1) * SSD_STATE]
            c_g = xbc_ref[rows, D_SSD + D_BC + g * SSD_STATE:D_SSD + D_BC + (g + 1) * SSD_STATE]
            y_diag = jnp.concatenate(diag[g * per_group:(g + 1) * per_group], axis=1)
            gate = z_ref[rows, hs].astype(F32)
            ea_exp, w_exp = p["exp"][g]
            y_off = jnp.dot(c_g, state[g].astype(BF16), preferred_element_type=F32) * ea_exp
            o_ref[rows, hs] = _rms((y_diag + y_off) * gate, ng_ref[:, hs]).astype(BF16)
            _state_step(state, g, b_g, xbc_ref[rows, hs], w_exp, ea_exp[CHUNK - 1:CHUNK, :])

    preps = [prep(ci) for ci in range(SSD_STEP_CHUNKS)]
    items = [(ci, k) for ci in range(SSD_STEP_CHUNKS) for k in range(0, N_SSD_HEADS, 2)]
    acols, ms, diag = {}, [], []
    for n in range(len(items) + FWD_LOOKAHEAD):
        if n < len(items):
            ci, k = items[n]
            acols[n] = acol_pair(preps[ci], k)
            if k == N_SSD_HEADS // 2:
                preps[ci]["exp"] = head_broadcasts(ci)
        m = n - FWD_LOOKAHEAD
        if m >= 0:
            ci, k = items[m]
            ms.extend(m_pair(preps[ci], k, acols.pop(m)))
            if len(ms) == DIAG_HEADS:
                diag.append(diag_block(ci, k + 2 - DIAG_HEADS, ms))
                ms = []
            if k + 2 == N_SSD_HEADS:
                carried(ci, preps[ci], diag)
                diag = []


def _ssd(xbcs, dt, gate, dt_bias, a_log, dskip, norm_g, consts, batch, seq):
    t = xbcs.shape[0]
    nblk = seq // SSD_BLOCK
    tri, ecol, e64, eye2, headmask = (consts[n] for n in ("tri", "ecol", "e64", "eye2", "headmask"))
    fwd = lambda b, c: (b * nblk + c, 0)
    bwd = lambda b, c: (b * nblk + (nblk - 1 - c), 0)
    state = pltpu.VMEM((SSD_GROUPS, SSD_STATE, D_SSD // SSD_GROUPS), F32)
    yb = pl.pallas_call(
        _ssd_bwd_kernel,
        grid=(batch, nblk),
        in_specs=[pl.BlockSpec((SSD_BLOCK, D_XBC), bwd), pl.BlockSpec((SSD_BLOCK, LANES), bwd),
                  _const_spec(dt_bias.shape), _const_spec(a_log.shape),
                  _const_spec(tri.shape), _const_spec(e64.shape)],
        out_specs=pl.BlockSpec((SSD_BLOCK, D_SSD), bwd),
        out_shape=jax.ShapeDtypeStruct((t, D_SSD), BF16),
        scratch_shapes=[state],
        compiler_params=_params(("parallel", "arbitrary")),
        name="ssd_bwd",
    )(xbcs, dt, dt_bias, a_log, tri, e64)
    return pl.pallas_call(
        _ssd_fwd_kernel,
        grid=(batch, nblk),
        in_specs=[pl.BlockSpec((SSD_BLOCK, D_XBC), fwd), pl.BlockSpec((SSD_BLOCK, LANES), fwd),
                  pl.BlockSpec((SSD_BLOCK, D_SSD), fwd), pl.BlockSpec((SSD_BLOCK, D_SSD), fwd),
                  _const_spec(dt_bias.shape), _const_spec(a_log.shape),
                  _const_spec(dskip.shape), _const_spec(norm_g.shape),
                  _const_spec(tri.shape), _const_spec(ecol.shape), _const_spec(e64.shape),
                  _const_spec(eye2.shape), _const_spec(headmask.shape)],
        out_specs=pl.BlockSpec((SSD_BLOCK, D_SSD), fwd),
        out_shape=jax.ShapeDtypeStruct((t, D_SSD), BF16),
        scratch_shapes=[state, pltpu.VMEM((SSD_STEP_CHUNKS, 3, CHUNK, LANES), BF16)],
        compiler_params=_params(("parallel", "arbitrary")),
        name="ssd_fwd",
    )(xbcs, dt, gate, yb, dt_bias, a_log, dskip, norm_g, tri, ecol, e64, eye2, headmask)


def _mix_ffn_kernel(nblk, xp_ref, xc_ref, xn_ref, ap_ref, ac_ref, an_ref, sp_ref, sc_ref, sn_ref,
                    wa_ref, ws_ref, gpost_ref, gpre_ref, wg_ref, wu_ref, wd_ref, cw_ref, cb_ref,
                    gffn_ref, o_ref, xext, aext, sext, hext, gbuf, act):
    i = pl.program_id(0) % nblk
    main = slice(HALO, HALO + TOK_BLOCK)
    for ext, (p_ref, c_ref, n_ref) in ((xext, (xp_ref, xc_ref, xn_ref)), (aext, (ap_ref, ac_ref, an_ref)),
                                       (sext, (sp_ref, sc_ref, sn_ref))):
        ext[0:HALO, :] = p_ref[...]
        ext[main, :] = c_ref[...]
        ext[HALO + TOK_BLOCK:, :] = n_ref[...]
    ext_half = (TOK_BLOCK + 2 * HALO) // 2
    for part in range(2):
        rows = slice(part * ext_half, (part + 1) * ext_half)
        mix = (jnp.dot(aext[rows, :], wa_ref[...], preferred_element_type=F32)
               + jnp.dot(sext[rows, :], ws_ref[...], preferred_element_type=F32))
        x1 = xext[rows, :] + _rms(mix, gpost_ref[...])
        xext[rows, :] = x1
        hext[rows, :] = _rms(x1, gpre_ref[...]).astype(BF16)
    keep_prev = jnp.where(i == 0, 0.0, 1.0)
    keep_next = jnp.where(i == nblk - 1, 0.0, 1.0)
    row = lax.broadcasted_iota(jnp.int32, (TOK_BLOCK, FF_COLS), 0)
    for c in range(D_FF // FF_COLS):
        cols = slice(c * FF_COLS, (c + 1) * FF_COLS)
        gbuf[...] = jnp.dot(hext[...], wg_ref[:, cols], preferred_element_type=F32)
        up = jnp.dot(hext[main, :], wu_ref[:, cols], preferred_element_type=F32)
        gp = gbuf[HALO - 1:HALO - 1 + TOK_BLOCK, :]
        gc = gbuf[main, :]
        gn = gbuf[HALO + 1:HALO + 1 + TOK_BLOCK, :]
        gp = jnp.where(row == 0, gp * keep_prev, gp)
        gn = jnp.where(row == TOK_BLOCK - 1, gn * keep_next, gn)
        gate = (gp * cw_ref[0:1, cols] + gc * cw_ref[1:2, cols] + gn * cw_ref[2:3, cols]
                + cb_ref[:, cols])
        gelu = 0.5 * gate * (1.0 + jnp.tanh(0.7978845608028654 * (gate + 0.044715 * gate * gate * gate)))
        act[:, cols] = (gelu * up).astype(BF16)
    for part in range(2):
        rows = slice(part * (TOK_BLOCK // 2), (part + 1) * (TOK_BLOCK // 2))
        f = jnp.dot(act[rows, :], wd_ref[...], preferred_element_type=F32)
        o_ref[rows, :] = xext[HALO + rows.start:HALO + rows.stop, :] + _rms(f, gffn_ref[...])


def _mix_ffn(x2, att, ssd, p, seq):
    t = x2.shape[0]
    nblk = seq // TOK_BLOCK
    per = TOK_BLOCK // HALO
    last = t // HALO - 1
    row = lambda i: (i, 0)
    prev = lambda i: (jnp.maximum(i * per - 1, 0), 0)
    nxt = lambda i: (jnp.minimum((i + 1) * per, last), 0)

    def halo_specs(width):
        return [pl.BlockSpec((HALO, width), prev), pl.BlockSpec((TOK_BLOCK, width), row),
                pl.BlockSpec((HALO, width), nxt)]

    weights = [p[n] for n in ("wo_a", "wo_s", "g_mix_post", "g_ffn_pre", "wg", "wu", "wd",
                              "ffn_cw", "ffn_cb", "g_ffn_post")]
    ext_rows = TOK_BLOCK + 2 * HALO
    return pl.pallas_call(
        functools.partial(_mix_ffn_kernel, nblk),
        grid=(t // TOK_BLOCK,),
        in_specs=(halo_specs(D_MODEL) + halo_specs(D_ATTN) + halo_specs(D_SSD)
                  + [_resident_spec(w.shape) for w in weights]),
        out_specs=pl.BlockSpec((TOK_BLOCK, D_MODEL), row),
        out_shape=jax.ShapeDtypeStruct((t, D_MODEL), F32),
        scratch_shapes=[pltpu.VMEM((ext_rows, D_MODEL), F32), pltpu.VMEM((ext_rows, D_ATTN), BF16),
                        pltpu.VMEM((ext_rows, D_SSD), BF16), pltpu.VMEM((ext_rows, D_MODEL), BF16),
                        pltpu.VMEM((ext_rows, FF_COLS), F32), pltpu.VMEM((TOK_BLOCK, D_FF), BF16)],
        compiler_params=_params(("parallel",)),
        name="mix_ffn",
    )(x2, x2, x2, att, att, att, ssd, ssd, ssd, *weights)


def _dt_tile(v):
    return jnp.tile(v, (1,) * (v.ndim - 1) + (DT_REP,))


def _prepare(norm_mix_pre, norm_mix_post, w_in, attn_rpb, attn_out_norm, ssd_conv_w, ssd_conv_b,
             ssd_dt_bias, ssd_a_log, ssd_d, ssd_norm, w_out, norm_ffn_pre, norm_ffn_post,
             w_gate, w_up, ffn_conv_w, ffn_conv_b, w_down):
    row = lambda v: v.reshape(1, -1).astype(F32)
    o_z = 3 * D_ATTN
    o_x = o_z + D_SSD
    o_dt = o_x + D_XBC
    scale = ATTN_HEAD_DIM ** -0.5
    wqkv = jnp.concatenate([w_in[:, :D_ATTN] * scale, w_in[:, D_ATTN:o_z]], axis=1)
    return dict(
        g_mix_pre=row(norm_mix_pre), g_mix_post=row(norm_mix_post),
        wqkv=wqkv.astype(BF16), wz=w_in[:, o_z:o_x].astype(BF16), wxbc=w_in[:, o_x:o_dt].astype(BF16),
        wdt=_dt_tile(w_in[:, o_dt:]).astype(BF16),
        bias_tab=_attn_bias_table(attn_rpb), g_attn=row(attn_out_norm),
        conv_w=ssd_conv_w.astype(F32), conv_b=row(ssd_conv_b),
        dt_bias=_dt_tile(row(ssd_dt_bias)), a_log=_dt_tile(row(ssd_a_log)),
        dskip=row(jnp.repeat(ssd_d, SSD_HEAD_DIM)), g_ssd=row(ssd_norm),
        wo_a=w_out[:D_ATTN].astype(BF16), wo_s=w_out[D_ATTN:].astype(BF16),
        g_ffn_pre=row(norm_ffn_pre), g_ffn_post=row(norm_ffn_post),
        wg=w_gate.astype(BF16), wu=w_up.astype(BF16), wd=w_down.astype(BF16),
        ffn_cw=ffn_conv_w.astype(F32), ffn_cb=row(ffn_conv_b),
    )


def _layer(x, p, consts):
    batch, seq, _ = x.shape
    x2 = x.reshape(batch * seq, D_MODEL)
    qkv, gate, xbcs, dt = _inproj(x2, p, seq)
    att = _attention(qkv, p["bias_tab"], p["g_attn"], batch, seq)
    ssd = _ssd(xbcs, dt, gate, p["dt_bias"], p["a_log"], p["dskip"], p["g_ssd"], consts, batch, seq)
    y = _mix_ffn(x2, att, ssd, p, seq)
    return y.reshape(batch, seq, D_MODEL)


def kernel(x_prompt, x_sample, norm_mix_pre, norm_mix_post, w_in, attn_rpb, attn_out_norm, ssd_conv_w, ssd_conv_b, ssd_dt_bias, ssd_a_log, ssd_d, ssd_norm, w_out, norm_ffn_pre, norm_ffn_post, w_gate, w_up, ffn_conv_w, ffn_conv_b, w_down):
    assert norm_mix_pre.shape[0] == 1, "single-layer trunk"
    p = _prepare(norm_mix_pre[0], norm_mix_post[0], w_in[0], attn_rpb[0], attn_out_norm[0],
                 ssd_conv_w[0], ssd_conv_b[0], ssd_dt_bias[0], ssd_a_log[0], ssd_d[0], ssd_norm[0],
                 w_out[0], norm_ffn_pre[0], norm_ffn_post[0], w_gate[0], w_up[0],
                 ffn_conv_w[0], ffn_conv_b[0], w_down[0])
    consts = _ssd_constants()
    return (_layer(x_prompt, p, consts), _layer(x_sample, p, consts))
```
